```python
import math
import jax
import jax.numpy as jnp
from jax import lax
import numpy as np


D_MODEL = 2048
BATCH = 16
SEQ = 2048
DEPTH = 2

CTX_LEN = 256
GRID_W = 64
EPS = 1e-6
N_BRANCH = 3
BRANCH_W = D_MODEL // 2

ATT_HD = 128
ATT_HEADS = BRANCH_W // ATT_HD
ATT_KV_HEADS = ATT_HEADS // 4
ATT_REP = ATT_HEADS // ATT_KV_HEADS
ATT_W = ATT_HEADS * ATT_HD
ATT_KV_W = ATT_KV_HEADS * ATT_HD
ATT_WIN = 128
ATT_BLOCK = 128
ATT_NSIDE = -(-ATT_WIN // ATT_BLOCK)
ATT_PAD = ATT_NSIDE * ATT_BLOCK
ATT_WIN_K = ATT_BLOCK + 2 * ATT_PAD
ATT_SCALE = ATT_HD ** -0.5
ROPE_BASE = 10000.0

ML_HEADS = 4
ML_HD = BRANCH_W // ML_HEADS
ML_W = ML_HEADS * ML_HD
ML_CHUNK = 64
ML_F_BIAS_LO = 3.0
ML_F_BIAS_HI = 6.0

SSM_HD = 64
SSM_HEADS = BRANCH_W // SSM_HD
SSM_GROUPS = 2
SSM_HPG = SSM_HEADS // SSM_GROUPS
SSM_STATE = 128
SSM_W = SSM_HEADS * SSM_HD
SSM_BC_W = SSM_GROUPS * SSM_STATE
SSM_CONV_CH = SSM_W + 2 * SSM_BC_W
SSM_CONV = 5
SSM_CHUNK = 64

IN_COLS = (('att_q', ATT_W), ('att_k', ATT_KV_W), ('att_v', ATT_KV_W), ('att_z', ATT_W),
           ('ml_q', ML_W), ('ml_k', ML_W), ('ml_v', ML_W), ('ml_o', ML_W), ('ml_z', ML_W),
           ('ml_gates', 4 * ML_HEADS),
           ('ssm_xbc', SSM_CONV_CH), ('ssm_dt', 2 * SSM_HEADS), ('ssm_z', SSM_W))
N_IN = 2 * ATT_W + 2 * ATT_KV_W + 5 * ML_W + 4 * ML_HEADS + SSM_CONV_CH + 2 * SSM_HEADS + SSM_W

kernel_name = 'hybrid_gated_swa_mlstm_ssd_dit'


def _rmsnorm(t, g):
    tf = t.astype(jnp.float32)
    tf = tf * lax.rsqrt(jnp.mean(tf * tf, -1, keepdims=True) + EPS)
    return (tf * g.astype(jnp.float32)).astype(t.dtype)


def _headnorm(t):
    return t * lax.rsqrt(jnp.mean(t * t, -1, keepdims=True) + EPS)


def _split_cols(u):
    out = {}
    off = 0
    for name, w in IN_COLS:
        out[name] = u[..., off:off + w]
        off += w
    return out


def _rev(t):
    return jnp.flip(t, axis=1)


def _chunks(t, ch):
    b, l = t.shape[:2]
    return jnp.moveaxis(t.reshape(b, l // ch, ch, *t.shape[2:]), 1, 0)


def _unchunks(t):
    t = jnp.moveaxis(t, 0, 1)
    return t.reshape(t.shape[0], t.shape[1] * t.shape[2], *t.shape[3:])


def _dwconv_centred(t, w, b):
    pad = w.shape[0] // 2
    y = lax.conv_general_dilated(t, w.astype(t.dtype)[:, None, :], window_strides=(1,),
                                 padding=[(pad, pad)], dimension_numbers=('NWC', 'WIO', 'NWC'),
                                 feature_group_count=t.shape[-1])
    return y + b.astype(t.dtype)


def _axial_rope_tables(row, col):
    n_freq = ATT_HD // 4
    inv = ROPE_BASE ** (-jnp.arange(n_freq, dtype=jnp.float32) / n_freq)
    ang = jnp.concatenate([row.astype(jnp.float32)[:, None] * inv,
                           col.astype(jnp.float32)[:, None] * inv], -1)
    return jnp.cos(ang), jnp.sin(ang)


def _rope(t, cos, sin):
    tf = t.astype(jnp.float32).reshape(*t.shape[:-1], ATT_HD // 2, 2)
    t0, t1 = tf[..., 0], tf[..., 1]
    cs = cos[None, :, None, :]
    sn = sin[None, :, None, :]
    out = jnp.stack([t0 * cs - t1 * sn, t0 * sn + t1 * cs], -1)
    return out.reshape(t.shape).astype(t.dtype)


def _sink_softmax(s, sink):
    m = jnp.maximum(jnp.max(s, -1, keepdims=True), sink)
    e = jnp.exp(s - m)
    return e / (jnp.sum(e, -1, keepdims=True) + jnp.exp(sink - m))


def _window_attention(q, k, v, kc, vc, sink_g):
    b_, s_ = q.shape[:2]
    nb = s_ // ATT_BLOCK
    qg = q.reshape(b_, s_, ATT_KV_HEADS, ATT_REP, ATT_HD)
    pad = ((0, 0), (ATT_PAD, ATT_PAD), (0, 0), (0, 0))
    kp = jnp.pad(k, pad)
    vp = jnp.pad(v, pad)
    sink_b = sink_g[None, :, :, None, None]

    def block(i):
        start = i * ATT_BLOCK
        qb = lax.dynamic_slice_in_dim(qg, start, ATT_BLOCK, axis=1)
        kb = lax.dynamic_slice_in_dim(kp, start, ATT_WIN_K, axis=1)
        vb = lax.dynamic_slice_in_dim(vp, start, ATT_WIN_K, axis=1)
        pos_q = start + jnp.arange(ATT_BLOCK)
        pos_k = start - ATT_PAD + jnp.arange(ATT_WIN_K)
        valid = ((jnp.abs(pos_q[:, None] - pos_k[None, :]) <= ATT_WIN)
                 & (pos_k >= 0)[None, :] & (pos_k < s_)[None, :])
        s_loc = jnp.einsum('bqgrd,bkgd->bgrqk', qb, kb).astype(jnp.float32) * ATT_SCALE
        s_loc = jnp.where(valid, s_loc, -jnp.inf)
        s_ctx = jnp.einsum('bqgrd,bkgd->bgrqk', qb, kc).astype(jnp.float32) * ATT_SCALE
        p = _sink_softmax(jnp.concatenate([s_loc, s_ctx], -1), sink_b).astype(v.dtype)
        return (jnp.einsum('bgrqk,bkgd->bqgrd', p[..., :ATT_WIN_K], vb)
                + jnp.einsum('bgrqk,bkgd->bqgrd', p[..., ATT_WIN_K:], vc))

    o = lax.map(block, jnp.arange(nb))
    return jnp.moveaxis(o, 0, 1).reshape(b_, s_, ATT_W)


def _ctx_attention(qc, kc, vc, sink_g):
    b_, l_ = qc.shape[:2]
    qg = qc.reshape(b_, l_, ATT_KV_HEADS, ATT_REP, ATT_HD)
    s = jnp.einsum('bqgrd,bkgd->bgrqk', qg, kc).astype(jnp.float32) * ATT_SCALE
    p = _sink_softmax(s, sink_g[None, :, :, None, None]).astype(vc.dtype)
    return jnp.einsum('bgrqk,bkgd->bqgrd', p, vc).reshape(b_, l_, ATT_W)


def _attention_branch(u, uc, sink, cos, sin, need_ctx):
    b_, s_ = u['att_q'].shape[:2]
    lc = uc['att_q'].shape[1]
    q = _rope(u['att_q'].reshape(b_, s_, ATT_HEADS, ATT_HD), cos, sin)
    k = _rope(u['att_k'].reshape(b_, s_, ATT_KV_HEADS, ATT_HD), cos, sin)
    v = u['att_v'].reshape(b_, s_, ATT_KV_HEADS, ATT_HD)
    kc = uc['att_k'].reshape(b_, lc, ATT_KV_HEADS, ATT_HD)
    vc = uc['att_v'].reshape(b_, lc, ATT_KV_HEADS, ATT_HD)
    sink_g = sink.astype(jnp.float32).reshape(ATT_KV_HEADS, ATT_REP)
    o = _window_attention(q, k, v, kc, vc, sink_g) * jax.nn.silu(u['att_z'])
    oc = None
    if need_ctx:
        qc = uc['att_q'].reshape(b_, lc, ATT_HEADS, ATT_HD)
        oc = _ctx_attention(qc, kc, vc, sink_g) * jax.nn.silu(uc['att_z'])
    return o, oc


def _mlstm_scan(q, k, v, ig, lf, state):
    tri = jnp.tril(jnp.ones((ML_CHUNK, ML_CHUNK), bool))

    def step(carry, inp):
        c_prev, n_prev, m_prev = carry
        qc, kc, vc, ic, fc = inp
        bcum = jnp.cumsum(fc, axis=1)
        inter = bcum + m_prev[:, None, :]
        dmat = bcum[:, :, None, :] - bcum[:, None, :, :] + ic[:, None, :, :]
        dmat = jnp.where(tri[None, :, :, None], dmat, -jnp.inf)
        mt = jnp.maximum(inter, jnp.max(dmat, axis=2))
        w = jnp.exp(dmat - mt[:, :, None, :])
        qk = jnp.einsum('bthd,bshd->btsh', qc, kc).astype(jnp.float32) * w
        e_int = jnp.exp(inter - mt)
        num = (e_int[..., None] * jnp.einsum('bthd,bhde->bthe', qc, c_prev)
               + jnp.einsum('btsh,bshe->bthe', qk, vc))
        den = e_int * jnp.einsum('bthd,bhd->bth', qc, n_prev) + jnp.sum(qk, axis=2)
        h = num / jnp.maximum(jnp.abs(den), jnp.exp(-mt))[..., None]
        m_new = mt[:, -1]
        wdec = jnp.exp(bcum[:, -1:] - bcum + ic - m_new[:, None, :])
        sp = jnp.exp(bcum[:, -1] + m_prev - m_new)
        c_new = sp[..., None, None] * c_prev + jnp.einsum('bsh,bshd,bshe->bhde', wdec, kc, vc)
        n_new = sp[..., None] * n_prev + jnp.einsum('bsh,bshd->bhd', wdec, kc)
        return (c_new, n_new, m_new), h

    xs = tuple(_chunks(t, ML_CHUNK) for t in (q, k, v, ig, lf))
    final, hs = lax.scan(step, state, xs)
    return final, _unchunks(hs)


def _mlstm_branch(u, uc, gate_bias, norm_g, need_ctx):
    gb = gate_bias.astype(jnp.float32)

    def prep(cols):
        b_, l_ = cols['ml_q'].shape[:2]
        hd = lambda t: t.reshape(b_, l_, ML_HEADS, ML_HD)
        q = hd(cols['ml_q'])
        k = hd(cols['ml_k']) * (ML_HD ** -0.5)
        v = hd(cols['ml_v'])
        g = cols['ml_gates'].astype(jnp.float32).reshape(b_, l_, 4, ML_HEADS) + gb
        return (q, k, v, g[:, :, 0], jax.nn.log_sigmoid(g[:, :, 1]),
                g[:, :, 2], jax.nn.log_sigmoid(g[:, :, 3]))

    def finish(cols, h_sum):
        b_, l_ = cols['ml_o'].shape[:2]
        o = jax.nn.sigmoid(cols['ml_o'].astype(jnp.float32)).reshape(b_, l_, ML_HEADS, ML_HD)
        hn = _headnorm(o * h_sum).reshape(b_, l_, ML_W)
        out = hn * norm_g.astype(jnp.float32) * jax.nn.silu(cols['ml_z'].astype(jnp.float32))
        return out.astype(cols['ml_z'].dtype)

    qc, kc, vc, igf_c, lff_c, igb_c, lfb_c = prep(uc)
    b_ = qc.shape[0]
    st0 = (jnp.zeros((b_, ML_HEADS, ML_HD, ML_HD), jnp.float32),
           jnp.zeros((b_, ML_HEADS, ML_HD), jnp.float32),
           jnp.zeros((b_, ML_HEADS), jnp.float32))
    st_f, hc_f = _mlstm_scan(qc, kc, vc, igf_c, lff_c, st0)
    st_b, hc_b = _mlstm_scan(_rev(qc), _rev(kc), _rev(vc), _rev(igb_c), _rev(lfb_c), st0)
    q, k, v, igf, lff, igb, lfb = prep(u)
    _, h_f = _mlstm_scan(q, k, v, igf, lff, st_f)
    _, h_b = _mlstm_scan(_rev(q), _rev(k), _rev(v), _rev(igb), _rev(lfb), st_b)
    out = finish(u, h_f + _rev(h_b))
    out_c = finish(uc, hc_f + _rev(hc_b)) if need_ctx else None
    return out, out_c


def _ssd_scan(xs, dt, a, bm, cm, h0):
    tri = jnp.tril(jnp.ones((SSM_CHUNK, SSM_CHUNK), bool))

    def step(h, inp):
        xc, dtc, bc, cc = inp
        cum = jnp.cumsum(dtc * a, axis=1)
        seg = cum[:, :, None] - cum[:, None, :]
        lmat = jnp.exp(jnp.where(tri[None, :, :, None, None], seg, -jnp.inf))
        cb = jnp.einsum('btgn,bsgn->btsg', cc, bc).astype(jnp.float32)
        mix = cb[..., None] * lmat * dtc[:, None]
        y = (jnp.einsum('btsgr,bsgrp->btgrp', mix, xc)
             + jnp.exp(cum)[..., None] * jnp.einsum('btgn,bgrnp->btgrp', cc, h))
        wdec = jnp.exp(cum[:, -1:] - cum) * dtc
        h_new = (jnp.exp(cum[:, -1])[..., None, None] * h
                 + jnp.einsum('bsgr,bsgn,bsgrp->bgrnp', wdec, bc, xc))
        return h_new, y

    seqs = tuple(_chunks(t, SSM_CHUNK) for t in (xs, dt, bm, cm))
    final, ys = lax.scan(step, h0, seqs)
    return final, _unchunks(ys)


def _ssd_branch(u, uc, conv_w, conv_b, dt_bias, a_log, d_skip, norm_g, need_ctx):
    a = (-jnp.exp(a_log.astype(jnp.float32))).reshape(2, SSM_GROUPS, SSM_HPG)
    dsk = d_skip.astype(jnp.float32).reshape(SSM_GROUPS, SSM_HPG)

    def prep(cols):
        b_, l_ = cols['ssm_xbc'].shape[:2]
        xbc = jax.nn.silu(_dwconv_centred(cols['ssm_xbc'], conv_w, conv_b))
        xs = xbc[..., :SSM_W].reshape(b_, l_, SSM_GROUPS, SSM_HPG, SSM_HD)
        bm = xbc[..., SSM_W:SSM_W + SSM_BC_W].reshape(b_, l_, SSM_GROUPS, SSM_STATE)
        cm = xbc[..., SSM_W + SSM_BC_W:].reshape(b_, l_, SSM_GROUPS, SSM_STATE)
        dt = jax.nn.softplus(cols['ssm_dt'].astype(jnp.float32).reshape(b_, l_, 2, SSM_HEADS)
                             + dt_bias.astype(jnp.float32))
        dt = dt.reshape(b_, l_, 2, SSM_GROUPS, SSM_HPG)
        return xs, bm, cm, dt[:, :, 0], dt[:, :, 1]

    def finish(cols, xs, y_sum):
        b_, l_ = xs.shape[:2]
        y = (y_sum + dsk[..., None] * xs.astype(jnp.float32)).reshape(b_, l_, SSM_W)
        y = y * jax.nn.silu(cols['ssm_z'].astype(jnp.float32))
        return _rmsnorm(y, norm_g).astype(cols['ssm_z'].dtype)

    xc, bc, cc, dtf_c, dtb_c = prep(uc)
    h0 = jnp.zeros((xc.shape[0], SSM_GROUPS, SSM_HPG, SSM_STATE, SSM_HD), jnp.float32)
    hf, yc_f = _ssd_scan(xc, dtf_c, a[0], bc, cc, h0)
    hb, yc_b = _ssd_scan(_rev(xc), _rev(dtb_c), a[1], _rev(bc), _rev(cc), h0)
    xs, bm, cm, dtf, dtb = prep(u)
    _, y_f = _ssd_scan(xs, dtf, a[0], bm, cm, hf)
    _, y_b = _ssd_scan(_rev(xs), _rev(dtb), a[1], _rev(bm), _rev(cm), hb)
    out = finish(u, xs, y_f + _rev(y_b))
    out_c = finish(uc, xc, yc_f + _rev(yc_b)) if need_ctx else None
    return out, out_c


def _merge(h, outs, w_gate, b_gate, w_branch, w_out):
    gates = jax.nn.sigmoid(h @ w_gate + b_gate).reshape(*h.shape[:-1], N_BRANCH, D_MODEL)
    proj = jnp.einsum('...nw,nwd->...nd', jnp.stack(outs, -2), w_branch)
    return jnp.sum(gates * proj, -2) @ w_out


def setup_inputs(seed: int = 0) -> dict:
    key = jax.random.key(seed)
    ks = jax.random.split(key, 24)
    f32 = jnp.float32

    def nrm(k, shape, scale):
        return jax.random.normal(k, shape, f32) * scale

    x = nrm(ks[0], (BATCH, SEQ, D_MODEL), 1.0)
    c = nrm(ks[1], (BATCH, D_MODEL), 1.0)
    ctx = nrm(ks[2], (BATCH, CTX_LEN, D_MODEL), 1.0)
    c_ctx = nrm(ks[3], (D_MODEL,), 1.0)
    w_mod = nrm(ks[4], (DEPTH, D_MODEL, 3 * D_MODEL), 0.5 * D_MODEL ** -0.5)
    b_mod = nrm(ks[5], (DEPTH, 3 * D_MODEL), 0.02)
    g_pre = 1.0 + nrm(ks[6], (DEPTH, D_MODEL), 0.02)
    w_in = nrm(ks[7], (DEPTH, D_MODEL, N_IN), D_MODEL ** -0.5)
    w_gate = nrm(ks[8], (DEPTH, D_MODEL, N_BRANCH * D_MODEL), D_MODEL ** -0.5)
    b_gate = nrm(ks[9], (DEPTH, N_BRANCH * D_MODEL), 0.02)
    att_sink = nrm(ks[10], (DEPTH, ATT_HEADS), 0.5)
    f_init = jnp.linspace(ML_F_BIAS_LO, ML_F_BIAS_HI, ML_HEADS, dtype=f32)
    zero_h = jnp.zeros((ML_HEADS,), f32)
    ml_gate_bias = (nrm(ks[11], (DEPTH, 4, ML_HEADS), 0.1)
                    + jnp.stack([zero_h, f_init, zero_h, f_init])[None])
    ml_norm = 1.0 + nrm(ks[12], (DEPTH, ML_W), 0.02)
    ssm_conv_w = nrm(ks[13], (DEPTH, SSM_CONV, SSM_CONV_CH), SSM_CONV ** -0.5)
    ssm_conv_b = nrm(ks[14], (DEPTH, SSM_CONV_CH), 0.02)
    dt0 = jnp.exp(jax.random.uniform(ks[15], (DEPTH, 2, SSM_HEADS), f32,
                                     math.log(1e-3), math.log(1e-1)))
    ssm_dt_bias = dt0 + jnp.log(-jnp.expm1(-dt0))
    ssm_a_log = jnp.log(jax.random.uniform(ks[16], (DEPTH, 2, SSM_HEADS), f32, 1.0, 16.0))
    ssm_d = 1.0 + nrm(ks[17], (DEPTH, SSM_HEADS), 0.1)
    ssm_norm = 1.0 + nrm(ks[18], (DEPTH, SSM_W), 0.02)
    w_branch = nrm(ks[19], (DEPTH, N_BRANCH, BRANCH_W, D_MODEL), BRANCH_W ** -0.5)
    w_out = nrm(ks[20], (DEPTH, D_MODEL, D_MODEL), D_MODEL ** -0.5)
    g_post = 1.0 + nrm(ks[21], (DEPTH, D_MODEL), 0.02)
    return {'x': x, 'c': c, 'ctx': ctx, 'c_ctx': c_ctx, 'w_mod': w_mod, 'b_mod': b_mod,
            'g_pre': g_pre, 'w_in': w_in, 'w_gate': w_gate, 'b_gate': b_gate,
            'att_sink': att_sink, 'ml_gate_bias': ml_gate_bias, 'ml_norm': ml_norm,
            'ssm_conv_w': ssm_conv_w, 'ssm_conv_b': ssm_conv_b, 'ssm_dt_bias': ssm_dt_bias,
            'ssm_a_log': ssm_a_log, 'ssm_d': ssm_d, 'ssm_norm': ssm_norm,
            'w_branch': w_branch, 'w_out': w_out, 'g_post': g_post}


def reference(x, c, ctx, c_ctx, w_mod, b_mod, g_pre, w_in, w_gate, b_gate, att_sink,
              ml_gate_bias, ml_norm, ssm_conv_w, ssm_conv_b, ssm_dt_bias, ssm_a_log, ssm_d,
              ssm_norm, w_branch, w_out, g_post):
    seq = x.shape[1]
    rows = seq // GRID_W
    row = jnp.repeat(jnp.arange(rows), GRID_W)
    col = jnp.tile(jnp.arange(GRID_W), rows)
    cos, sin = _axial_rope_tables(row, col)
    xc = ctx
    sc = jax.nn.silu(c)
    scc = jax.nn.silu(c_ctx)
    for l in range(DEPTH):
        need_ctx = l < DEPTH - 1
        mod = sc @ w_mod[l] + b_mod[l]
        mod_c = scc @ w_mod[l] + b_mod[l]
        shift, scale, gate = jnp.split(mod[:, None, :], 3, axis=-1)
        shift_c, scale_c, gate_c = jnp.split(mod_c, 3, axis=-1)
        h = _rmsnorm(x, g_pre[l]) * (1.0 + scale) + shift
        hc = _rmsnorm(xc, g_pre[l]) * (1.0 + scale_c) + shift_c
        u = _split_cols(h @ w_in[l])
        uc = _split_cols(hc @ w_in[l])
        o_a, o_a_c = _attention_branch(u, uc, att_sink[l], cos, sin, need_ctx)
        o_m, o_m_c = _mlstm_branch(u, uc, ml_gate_bias[l], ml_norm[l], need_ctx)
        o_s, o_s_c = _ssd_branch(u, uc, ssm_conv_w[l], ssm_conv_b[l], ssm_dt_bias[l],
                                 ssm_a_log[l], ssm_d[l], ssm_norm[l], need_ctx)
        y = _merge(h, (o_a, o_m, o_s), w_gate[l], b_gate[l], w_branch[l], w_out[l])
        x = x + gate * _rmsnorm(y, g_post[l])
        if need_ctx:
            yc = _merge(hc, (o_a_c, o_m_c, o_s_c), w_gate[l], b_gate[l], w_branch[l], w_out[l])
            xc = xc + gate_c * _rmsnorm(yc, g_post[l])
    return x
```

```python
import functools

import numpy as np
import jax
import jax.numpy as jnp
from jax import lax
from jax.experimental import pallas as pl
from jax.experimental.pallas import tpu as pltpu

F32 = jnp.float32
BF16 = jnp.bfloat16
HIGHEST = lax.Precision.HIGHEST

D_MODEL = 2048
EPS = 1e-6
BRANCH_W = D_MODEL // 2
GRID_W = 64
ROPE_BASE = 10000.0

ATT_HD = 128
ATT_HEADS = BRANCH_W // ATT_HD
ATT_KV_HEADS = ATT_HEADS // 4
ATT_REP = ATT_HEADS // ATT_KV_HEADS
ATT_KV_W = ATT_KV_HEADS * ATT_HD
ATT_BLOCK = 128
ATT_SCALE = ATT_HD ** -0.5

ML_HEADS = 4
ML_HD = BRANCH_W // ML_HEADS

SSM_HD = 64
SSM_HEADS = BRANCH_W // SSM_HD
SSM_GROUPS = 2
SSM_HPG = SSM_HEADS // SSM_GROUPS
SSM_STATE = 128
SSM_BC_W = SSM_GROUPS * SSM_STATE
SSM_CONV = 5
SSM_CONV_CH = BRANCH_W + 2 * SSM_BC_W

LANES = 128
CHUNK = 128
NEG = -1e30
VMEM_LIMIT = 56 * 1024 * 1024

U_ATT_Q = 0
U_ATT_Z = 1024
U_ML_Q = 2048
U_ML_K = 3072
U_ML_V = 4096
U_ML_O = 5120
U_ML_Z = 6144
U_SSM_Z = 7168
U_SSM_X = 8192
U_SSM_B = 9216
U_SSM_C = 9472
U_ATT_K = 9728
U_ATT_V = 9984
U_GATE = 10240
U_COLS = U_GATE + 3 * D_MODEL
S_ML_GATES = 0
S_SSM_DT = 4 * ML_HEADS
S_COLS = LANES

IN_TN = 512


def _cparams(*sem):
    return pltpu.CompilerParams(dimension_semantics=sem, vmem_limit_bytes=VMEM_LIMIT)


def _sigmoid(v):
    return 1.0 / (1.0 + jnp.exp(-v))


def _silu(v):
    return v * _sigmoid(v)


def _softplus(v):
    return jnp.maximum(v, 0.0) + jnp.log1p(jnp.exp(-jnp.abs(v)))


def _log_sigmoid(v):
    return jnp.minimum(v, 0.0) - jnp.log1p(jnp.exp(-jnp.abs(v)))


def _dot(a, b):
    return jnp.dot(a, b, preferred_element_type=F32)


def _dot_nt(a, b):
    return lax.dot_general(a, b, (((1,), (1,)), ((), ())), preferred_element_type=F32)


def _dot_tn(a, b):
    return lax.dot_general(a, b, (((0,), (0,)), ((), ())), preferred_element_type=F32)


def _dot_exact(a, b):
    return lax.dot_general(a, b, (((1,), (0,)), ((), ())), precision=HIGHEST,
                           preferred_element_type=F32)


def _mod_kernel(c_ref, w_ref, b_ref, o_ref):
    sc = _silu(c_ref[...]).astype(BF16)
    o_ref[0] = _dot(sc, w_ref[0].astype(BF16)) + b_ref[0]


def _modulation(cc, w_mod, b_mod):
    depth, d, n = w_mod.shape
    rows = cc.shape[0]
    tn = 1024
    return pl.pallas_call(
        _mod_kernel,
        grid=(depth, n // tn),
        in_specs=[pl.BlockSpec((rows, d), lambda l, j: (0, 0)),
                  pl.BlockSpec((1, d, tn), lambda l, j: (l, 0, j)),
                  pl.BlockSpec((1, 1, tn), lambda l, j: (l, 0, j))],
        out_specs=pl.BlockSpec((1, rows, tn), lambda l, j: (l, 0, j)),
        out_shape=jax.ShapeDtypeStruct((depth, rows, n), F32),
        compiler_params=_cparams("arbitrary", "arbitrary"),
        name="modulation",
    )(cc, w_mod, b_mod.reshape(depth, 1, n))


def _in_proj_kernel(x_ref, mod_ref, g_ref, w_ref, ws_ref, u_ref, us_ref, h_scr):
    @pl.when(pl.program_id(2) == 0)
    def _():
        x = x_ref[0]
        ms = jnp.mean(x * x, axis=-1, keepdims=True)
        xn = x * lax.rsqrt(ms + EPS) * g_ref[...]
        h = xn * (1.0 + mod_ref[0, 1:2, :]) + mod_ref[0, 0:1, :]
        hb = h.astype(BF16)
        h_scr[...] = hb
        us_ref[0] = _dot(hb, ws_ref[...])

    u_ref[0] = _dot(h_scr[...], w_ref[...]).astype(BF16)


def _in_proj(x3, mod3, g_pre, w_big, w_small):
    g, m, d = x3.shape
    tm = min(1024, m)
    n = w_big.shape[1]
    return pl.pallas_call(
        _in_proj_kernel,
        grid=(g, m // tm, n // IN_TN),
        in_specs=[pl.BlockSpec((1, tm, d), lambda b, i, j: (b, i, 0)),
                  pl.BlockSpec((1, 3, d), lambda b, i, j: (b, 0, 0)),
                  pl.BlockSpec((1, d), lambda b, i, j: (0, 0)),
                  pl.BlockSpec((d, IN_TN), lambda b, i, j: (0, j)),
                  pl.BlockSpec((d, S_COLS), lambda b, i, j: (0, 0))],
        out_specs=[pl.BlockSpec((1, tm, IN_TN), lambda b, i, j: (b, i, j)),
                   pl.BlockSpec((1, tm, S_COLS), lambda b, i, j: (b, i, 0))],
        out_shape=[jax.ShapeDtypeStruct((g, m, n), BF16),
                   jax.ShapeDtypeStruct((g, m, S_COLS), F32)],
        scratch_shapes=[pltpu.VMEM((tm, d), BF16)],
        compiler_params=_cparams("arbitrary", "arbitrary", "arbitrary"),
        name="in_proj",
    )(x3, mod3, g_pre.reshape(1, d), w_big, w_small)


def _sink_attend(qh, kk, vv, bias, sink):
    s = _dot_nt(qh, kk)
    if bias is not None:
        s = s + bias
    m = jnp.maximum(jnp.max(s, axis=1, keepdims=True), sink)
    e = jnp.exp(s - m)
    den = jnp.sum(e, axis=1, keepdims=True) + jnp.exp(sink - m)
    return _dot(e.astype(BF16), vv) / den


def _att_kernel(sink_ref, q_ref, z_ref, kp_ref, kc_ref, kn_ref, vp_ref, vc_ref, vn_ref,
                ctxk_ref, ctxv_ref, cq_ref, sq_ref, ck_ref, sk_ref, o_ref, *, nb, lc):
    i = pl.program_id(1)
    blk = ATT_BLOCK
    r0 = pl.multiple_of(i * blk, blk)
    rp = pl.multiple_of(jnp.maximum(i - 1, 0) * blk, blk)
    rn = pl.multiple_of(jnp.minimum(i + 1, nb - 1) * blk, blk)

    def rope(t, cos2, sin2):
        return t * cos2 + pltpu.roll(t, ATT_HD // 2, 1) * sin2

    cq = cq_ref[pl.ds(r0, blk), :]
    sq = sq_ref[pl.ds(r0, blk), :]
    row = lax.broadcasted_iota(jnp.int32, (blk, blk), 0)
    col = lax.broadcasted_iota(jnp.int32, (blk, blk), 1)
    bias_p = jnp.where(col >= row, 0.0, NEG) + jnp.where(i > 0, 0.0, NEG)
    bias_n = jnp.where(col <= row, 0.0, NEG) + jnp.where(i < nb - 1, 0.0, NEG)
    bias = jnp.concatenate([bias_p, jnp.zeros((blk, blk), F32), bias_n,
                            jnp.zeros((blk, lc), F32)], axis=1)
    for g in range(ATT_KV_HEADS):
        ks = slice(g * ATT_HD, (g + 1) * ATT_HD)
        kparts = []
        for k_ref, r in ((kp_ref, rp), (kc_ref, r0), (kn_ref, rn)):
            kr = rope(k_ref[0][:, ks].astype(F32), ck_ref[pl.ds(r, blk), :], sk_ref[pl.ds(r, blk), :])
            kparts.append(kr.astype(BF16))
        kk = jnp.concatenate(kparts + [ctxk_ref[0][:, ks]], axis=0)
        vv = jnp.concatenate([vp_ref[0][:, ks], vc_ref[0][:, ks], vn_ref[0][:, ks],
                              ctxv_ref[0][:, ks]], axis=0)
        for r in range(ATT_REP):
            hh = g * ATT_REP + r
            hs = slice(hh * ATT_HD, (hh + 1) * ATT_HD)
            qh = rope(q_ref[0][:, hs].astype(F32), cq, sq).astype(BF16)
            o = _sink_attend(qh, kk, vv, bias, sink_ref[hh])
            o_ref[0, :, hs] = (o * _silu(z_ref[0][:, hs].astype(F32))).astype(BF16)


def _attention(u, uc, sink, tables):
    b, s, _ = u.shape
    lc = uc.shape[1]
    nb = s // ATT_BLOCK
    blk = ATT_BLOCK
    kcol = U_ATT_K // ATT_KV_W
    vcol = U_ATT_V // ATT_KV_W

    def kv_spec(col, shift):
        return pl.BlockSpec((1, blk, ATT_KV_W),
                            lambda bb, i: (bb, jnp.clip(i + shift, 0, nb - 1), col))

    tab_spec = pl.BlockSpec((s, ATT_HD), lambda bb, i: (0, 0))
    return pl.pallas_call(
        functools.partial(_att_kernel, nb=nb, lc=lc),
        grid=(b, nb),
        in_specs=[pl.BlockSpec(memory_space=pltpu.SMEM),
                  pl.BlockSpec((1, blk, BRANCH_W), lambda bb, i: (bb, i, U_ATT_Q // BRANCH_W)),
                  pl.BlockSpec((1, blk, BRANCH_W), lambda bb, i: (bb, i, U_ATT_Z // BRANCH_W)),
                  kv_spec(kcol, -1), kv_spec(kcol, 0), kv_spec(kcol, 1),
                  kv_spec(vcol, -1), kv_spec(vcol, 0), kv_spec(vcol, 1),
                  pl.BlockSpec((1, lc, ATT_KV_W), lambda bb, i: (bb, 0, kcol)),
                  pl.BlockSpec((1, lc, ATT_KV_W), lambda bb, i: (bb, 0, vcol)),
                  tab_spec, tab_spec, tab_spec, tab_spec],
        out_specs=pl.BlockSpec((1, blk, BRANCH_W), lambda bb, i: (bb, i, 0)),
        out_shape=jax.ShapeDtypeStruct((b, s, BRANCH_W), BF16),
        compiler_params=_cparams("arbitrary", "arbitrary"),
        name="window_attention",
    )(sink, u, u, u, u, u, u, u, u, uc, uc, *tables)


def _ctx_att_kernel(sink_ref, q_ref, z_ref, k_ref, v_ref, o_ref):
    for hh in range(ATT_HEADS):
        g = hh // ATT_REP
        ks = slice(g * ATT_HD, (g + 1) * ATT_HD)
        hs = slice(hh * ATT_HD, (hh + 1) * ATT_HD)
        qh = (q_ref[0][:, hs].astype(F32) * ATT_SCALE).astype(BF16)
        o = _sink_attend(qh, k_ref[0][:, ks], v_ref[0][:, ks], None, sink_ref[hh])
        o_ref[0, :, hs] = (o * _silu(z_ref[0][:, hs].astype(F32))).astype(BF16)


def _ctx_attention(uc, sink):
    b, lc, _ = uc.shape
    kcol = U_ATT_K // ATT_KV_W
    vcol = U_ATT_V // ATT_KV_W
    return pl.pallas_call(
        _ctx_att_kernel,
        grid=(b,),
        in_specs=[pl.BlockSpec(memory_space=pltpu.SMEM),
                  pl.BlockSpec((1, lc, BRANCH_W), lambda bb: (bb, 0, U_ATT_Q // BRANCH_W)),
                  pl.BlockSpec((1, lc, BRANCH_W), lambda bb: (bb, 0, U_ATT_Z // BRANCH_W)),
                  pl.BlockSpec((1, lc, ATT_KV_W), lambda bb: (bb, 0, kcol)),
                  pl.BlockSpec((1, lc, ATT_KV_W), lambda bb: (bb, 0, vcol))],
        out_specs=pl.BlockSpec((1, lc, BRANCH_W), lambda bb: (bb, 0, 0)),
        out_shape=jax.ShapeDtypeStruct((b, lc, BRANCH_W), BF16),
        compiler_params=_cparams("arbitrary"),
        name="context_attention",
    )(sink, uc, uc, uc, uc)


def _scan_masks(reverse, n):
    row = lax.broadcasted_iota(jnp.int32, (n, n), 0)
    col = lax.broadcasted_iota(jnp.int32, (n, n), 1)
    tri = (col >= row) if reverse else (col <= row)
    return tri, jnp.where(tri, 1.0, 0.0).astype(F32)


def _mlstm_dir(reverse, q_ref, k_ref, v_ref, g_ref, gbias_ref, c_ref, n_ref, m_ref):
    d = 1 if reverse else 0
    n = CHUNK
    last = 0 if reverse else n - 1
    tri, trif = _scan_masks(reverse, n)
    g_all = g_ref[0] + gbias_ref[...]
    b_all = _dot_exact(trif, _log_sigmoid(g_all))
    g_t = g_all.T
    b_t = b_all.T
    outs = []
    for h in range(ML_HEADS):
        ci = S_ML_GATES + 2 * ML_HEADS * d + h
        cf = ci + ML_HEADS
        idx = d * ML_HEADS + h
        hs = slice(h * ML_HD, (h + 1) * ML_HD)
        bcol = b_all[:, cf:cf + 1]
        brow = b_t[cf:cf + 1, :]
        icol = g_all[:, ci:ci + 1]
        irow = g_t[ci:ci + 1, :]
        m_prev = m_ref[0, idx:idx + 1, 0:1]
        n_prev = n_ref[0, idx:idx + 1, :]
        c_prev = c_ref[0, idx]
        qh = q_ref[0][:, hs]
        kh = k_ref[0][:, hs]
        vh = v_ref[0][:, hs]

        dmat = jnp.where(tri, bcol - brow + irow, NEG)
        inter = bcol + m_prev
        mt = jnp.maximum(inter, jnp.max(dmat, axis=1, keepdims=True))
        qk = _dot_nt(qh, kh) * jnp.exp(dmat - mt)
        e_int = jnp.exp(inter - mt)
        num = e_int * _dot(qh, c_prev.astype(BF16)) + _dot(qk.astype(BF16), vh)
        den = (e_int * jnp.sum(qh.astype(F32) * n_prev, axis=1, keepdims=True)
               + jnp.sum(qk, axis=1, keepdims=True))
        outs.append(num / jnp.maximum(jnp.abs(den), jnp.exp(-mt)))

        m_new = mt[last:last + 1, :]
        total = bcol[last:last + 1, :]
        kw = kh.astype(F32) * jnp.exp(total - bcol + icol - m_new)
        sp = jnp.exp(total + m_prev - m_new)
        c_ref[0, idx] = sp * c_prev + _dot_tn(kw.astype(BF16), vh)
        n_ref[0, idx:idx + 1, :] = sp * n_prev + jnp.sum(kw, axis=0, keepdims=True)
        m_ref[0, idx:idx + 1, :] = jnp.broadcast_to(m_new, (1, LANES))
    return jnp.concatenate(outs, axis=1)


def _mlstm_finish(h_sum, o_ref, z_ref, ng_ref):
    t = _sigmoid(o_ref[0].astype(F32)) * h_sum
    parts = []
    for h in range(ML_HEADS):
        th = t[:, h * ML_HD:(h + 1) * ML_HD]
        parts.append(th * lax.rsqrt(jnp.mean(th * th, axis=1, keepdims=True) + EPS))
    hn = jnp.concatenate(parts, axis=1)
    return (hn * ng_ref[...] * _silu(z_ref[0].astype(F32))).astype(BF16)


def _mlstm_kernel(qf, kf, vf, gf, of, zf, qb, kb, vb, gb, ob, zb, gbias, ng, c0, n0, m0,
                  out, cfin, nfin, mfin, h_scr, *, nc):
    s = pl.program_id(1)

    @pl.when(s == 0)
    def _():
        cfin[...] = c0[...]
        nfin[...] = n0[...]
        mfin[...] = m0[...]

    h_f = _mlstm_dir(False, qf, kf, vf, gf, gbias, cfin, nfin, mfin)
    h_b = _mlstm_dir(True, qb, kb, vb, gb, gbias, cfin, nfin, mfin)
    rf = pl.multiple_of(s * CHUNK, CHUNK)
    rb = pl.multiple_of((nc - 1 - s) * CHUNK, CHUNK)

    @pl.when(s < nc // 2)
    def _():
        h_scr[pl.ds(rf, CHUNK), :] = h_f
        h_scr[pl.ds(rb, CHUNK), :] = h_b

    @pl.when(s >= nc // 2)
    def _():
        out[0, pl.ds(rf, CHUNK), :] = _mlstm_finish(h_f + h_scr[pl.ds(rf, CHUNK), :], of, zf, ng)
        out[0, pl.ds(rb, CHUNK), :] = _mlstm_finish(h_b + h_scr[pl.ds(rb, CHUNK), :], ob, zb, ng)


def _mlstm(u, us, gbias_row, norm_g, state):
    b, t, _ = u.shape
    nc = t // CHUNK
    assert nc % 2 == 0
    c0, n0, m0 = state
    nst = 2 * ML_HEADS

    def fwd(col):
        return pl.BlockSpec((1, CHUNK, BRANCH_W), lambda bb, s: (bb, s, col // BRANCH_W))

    def bwd(col):
        return pl.BlockSpec((1, CHUNK, BRANCH_W), lambda bb, s: (bb, nc - 1 - s, col // BRANCH_W))

    gate_f = pl.BlockSpec((1, CHUNK, S_COLS), lambda bb, s: (bb, s, 0))
    gate_b = pl.BlockSpec((1, CHUNK, S_COLS), lambda bb, s: (bb, nc - 1 - s, 0))
    c_spec = pl.BlockSpec((1, nst, ML_HD, ML_HD), lambda bb, s: (bb, 0, 0, 0))
    n_spec = pl.BlockSpec((1, nst, ML_HD), lambda bb, s: (bb, 0, 0))
    m_spec = pl.BlockSpec((1, nst, LANES), lambda bb, s: (bb, 0, 0))
    out, cf, nf, mf = pl.pallas_call(
        functools.partial(_mlstm_kernel, nc=nc),
        grid=(b, nc),
        in_specs=[fwd(U_ML_Q), fwd(U_ML_K), fwd(U_ML_V), gate_f, fwd(U_ML_O), fwd(U_ML_Z),
                  bwd(U_ML_Q), bwd(U_ML_K), bwd(U_ML_V), gate_b, bwd(U_ML_O), bwd(U_ML_Z),
                  pl.BlockSpec((1, S_COLS), lambda bb, s: (0, 0)),
                  pl.BlockSpec((1, BRANCH_W), lambda bb, s: (0, 0)),
                  c_spec, n_spec, m_spec],
        out_specs=[pl.BlockSpec((1, t, BRANCH_W), lambda bb, s: (bb, 0, 0)),
                   c_spec, n_spec, m_spec],
        out_shape=[jax.ShapeDtypeStruct((b, t, BRANCH_W), BF16),
                   jax.ShapeDtypeStruct(c0.shape, F32),
                   jax.ShapeDtypeStruct(n0.shape, F32),
                   jax.ShapeDtypeStruct(m0.shape, F32)],
        scratch_shapes=[pltpu.VMEM((t, BRANCH_W), F32)],
        compiler_params=_cparams("arbitrary", "arbitrary"),
        name="mlstm",
    )(u, u, u, us, u, u, u, u, u, us, u, u, gbias_row, norm_g.reshape(1, BRANCH_W), c0, n0, m0)
    return out, (cf, nf, mf)


CONV_TN = 512
CONV_PAD = 16


def _conv_kernel(x_ref, w_ref, b_ref, o_ref, pad_scr, *, t):
    pad_scr[0:CONV_PAD, :] = jnp.zeros((CONV_PAD, CONV_TN), F32)
    pad_scr[CONV_PAD + t:2 * CONV_PAD + t, :] = jnp.zeros((CONV_PAD, CONV_TN), F32)
    pad_scr[CONV_PAD:CONV_PAD + t, :] = x_ref[0].astype(F32)
    rows = 256
    half = SSM_CONV // 2
    for c in range(t // rows):
        r0 = c * rows
        acc = jnp.broadcast_to(b_ref[...], (rows, CONV_TN))
        for k in range(SSM_CONV):
            lo = CONV_PAD + r0 + k - half
            acc = acc + w_ref[k:k + 1, :] * pad_scr[lo:lo + rows, :]
        o_ref[0, r0:r0 + rows, :] = _silu(acc).astype(BF16)


def _ssm_conv(u, conv_w, conv_b):
    b, t, _ = u.shape
    w8 = jnp.zeros((8, SSM_CONV_CH), F32).at[:SSM_CONV].set(conv_w)
    first = U_SSM_X // CONV_TN
    return pl.pallas_call(
        functools.partial(_conv_kernel, t=t),
        grid=(b, SSM_CONV_CH // CONV_TN),
        in_specs=[pl.BlockSpec((1, t, CONV_TN), lambda bb, j: (bb, 0, first + j)),
                  pl.BlockSpec((8, CONV_TN), lambda bb, j: (0, j)),
                  pl.BlockSpec((1, CONV_TN), lambda bb, j: (0, j))],
        out_specs=pl.BlockSpec((1, t, CONV_TN), lambda bb, j: (bb, 0, j)),
        out_shape=jax.ShapeDtypeStruct((b, t, SSM_CONV_CH), BF16),
        scratch_shapes=[pltpu.VMEM((t + 2 * CONV_PAD, CONV_TN), F32)],
        compiler_params=_cparams("arbitrary", "arbitrary"),
        name="ssm_conv",
    )(u, w8, conv_b.reshape(1, SSM_CONV_CH))


def _ssd_dir(reverse, x_ref, bm_ref, cm_ref, sm_ref, dtb_ref, a_ref, e_ref, even_ref, h_ref):
    d = 1 if reverse else 0
    n = CHUNK
    last = 0 if reverse else n - 1
    tri, trif = _scan_masks(reverse, n)
    dt_all = _softplus(sm_ref[0] + dtb_ref[...])
    cum = _dot_exact(trif, dt_all * a_ref[...])
    total = cum[last:last + 1, :]
    cum_t = cum.T
    e_mat = e_ref[d]

    def expand(v):
        hi = v.astype(BF16)
        lo = (v - hi.astype(F32)).astype(BF16)
        return _dot(hi, e_mat) + _dot(lo, e_mat)

    dt_x = expand(dt_all)
    dec_x = expand(jnp.exp(cum))
    wd_x = expand(jnp.exp(total - cum) * dt_all)
    tot_x = expand(jnp.broadcast_to(jnp.exp(total), (8, LANES)))[0:1, :]

    x = x_ref[0].astype(F32)
    xdt = x * dt_x
    even = even_ref[...]
    xdt_e = (xdt * even).astype(BF16)
    xdt_o = (xdt * (1.0 - even)).astype(BF16)
    xw = (x * wd_x).astype(BF16)
    gw = SSM_HPG * SSM_HD
    ys = []
    for g in range(SSM_GROUPS):
        bg = bm_ref[0][:, g * SSM_STATE:(g + 1) * SSM_STATE]
        cg = cm_ref[0][:, g * SSM_STATE:(g + 1) * SSM_STATE]
        cb = _dot_nt(cg, bg)
        h_prev = h_ref[0, d * SSM_GROUPS + g]
        y_inter = _dot(cg, h_prev.astype(BF16))
        for p in range(SSM_HPG // 2):
            acc = None
            for par, xsrc in ((0, xdt_e), (1, xdt_o)):
                hd = g * SSM_HPG + 2 * p + par
                c = S_SSM_DT + SSM_HEADS * d + hd
                seg = jnp.where(tri, cum[:, c:c + 1] - cum_t[c:c + 1, :], NEG)
                mix = (cb * jnp.exp(seg)).astype(BF16)
                lo = (g * SSM_HPG + 2 * p) * SSM_HD
                term = _dot(mix, xsrc[:, lo:lo + 2 * SSM_HD])
                acc = term if acc is None else acc + term
            ys.append(acc)
        gs = slice(g * gw, (g + 1) * gw)
        ys[-(SSM_HPG // 2):] = [jnp.concatenate(ys[-(SSM_HPG // 2):], axis=1) + dec_x[:, gs] * y_inter]
        h_ref[0, d * SSM_GROUPS + g] = tot_x[:, gs] * h_prev + _dot_tn(bg, xw[:, gs])
    return jnp.concatenate(ys, axis=1), x


def _ssd_finish(y_sum, x, z_ref, dsk_ref, ng_ref):
    y = (y_sum + dsk_ref[...] * x) * _silu(z_ref[0].astype(F32))
    y = y * lax.rsqrt(jnp.mean(y * y, axis=1, keepdims=True) + EPS)
    return (y * ng_ref[...]).astype(BF16)


def _ssd_kernel(xf, bf, cf, sf, zf, xb, bb_, cb_, sb, zb, dtb, arow, emat, even, dsk, ng, h0,
                out, hfin, y_scr, *, nc):
    s = pl.program_id(1)

    @pl.when(s == 0)
    def _():
        hfin[...] = h0[...]

    y_f, x_f = _ssd_dir(False, xf, bf, cf, sf, dtb, arow, emat, even, hfin)
    y_b, x_b = _ssd_dir(True, xb, bb_, cb_, sb, dtb, arow, emat, even, hfin)
    rf = pl.multiple_of(s * CHUNK, CHUNK)
    rb = pl.multiple_of((nc - 1 - s) * CHUNK, CHUNK)

    @pl.when(s < nc // 2)
    def _():
        y_scr[pl.ds(rf, CHUNK), :] = y_f
        y_scr[pl.ds(rb, CHUNK), :] = y_b

    @pl.when(s >= nc // 2)
    def _():
        out[0, pl.ds(rf, CHUNK), :] = _ssd_finish(y_f + y_scr[pl.ds(rf, CHUNK), :], x_f, zf, dsk, ng)
        out[0, pl.ds(rb, CHUNK), :] = _ssd_finish(y_b + y_scr[pl.ds(rb, CHUNK), :], x_b, zb, dsk, ng)


def _ssd(xbc, u, us, params, h0):
    b, t, _ = u.shape
    nc = t // CHUNK
    assert nc % 2 == 0
    dtb_row, a_row, e_mat, even_row, dsk_row, ng_row = params
    nst = 2 * SSM_GROUPS
    gw = SSM_HPG * SSM_HD

    def chunk_spec(width, col, rev):
        if rev:
            return pl.BlockSpec((1, CHUNK, width), lambda bb, s: (bb, nc - 1 - s, col))
        return pl.BlockSpec((1, CHUNK, width), lambda bb, s: (bb, s, col))

    def dir_specs(rev):
        return [chunk_spec(BRANCH_W, 0, rev),
                chunk_spec(SSM_BC_W, BRANCH_W // SSM_BC_W, rev),
                chunk_spec(SSM_BC_W, BRANCH_W // SSM_BC_W + 1, rev),
                chunk_spec(S_COLS, 0, rev),
                chunk_spec(BRANCH_W, U_SSM_Z // BRANCH_W, rev)]

    def row_spec(width):
        return pl.BlockSpec((1, width), lambda bb, s: (0, 0))

    h_spec = pl.BlockSpec((1, nst, SSM_STATE, gw), lambda bb, s: (bb, 0, 0, 0))
    out, hf = pl.pallas_call(
        functools.partial(_ssd_kernel, nc=nc),
        grid=(b, nc),
        in_specs=dir_specs(False) + dir_specs(True) + [
            row_spec(S_COLS), row_spec(S_COLS),
            pl.BlockSpec((2, S_COLS, BRANCH_W), lambda bb, s: (0, 0, 0)),
            row_spec(BRANCH_W), row_spec(BRANCH_W), row_spec(BRANCH_W), h_spec],
        out_specs=[pl.BlockSpec((1, t, BRANCH_W), lambda bb, s: (bb, 0, 0)), h_spec],
        out_shape=[jax.ShapeDtypeStruct((b, t, BRANCH_W), BF16),
                   jax.ShapeDtypeStruct(h0.shape, F32)],
        scratch_shapes=[pltpu.VMEM((t, BRANCH_W), F32)],
        compiler_params=_cparams("arbitrary", "arbitrary"),
        name="ssd",
    )(xbc, xbc, xbc, us, u, xbc, xbc, xbc, us, u,
      dtb_row, a_row, e_mat, even_row, dsk_row, ng_row, h0)
    return out, hf


MIX_TN = 512


def _branch_mix_kernel(oa_ref, om_ref, os_ref, wb_ref, ga_ref, gm_ref, gs_ref, bg_ref, out_ref):
    acc = None
    for n, (o_ref, g_ref) in enumerate(((oa_ref, ga_ref), (om_ref, gm_ref), (os_ref, gs_ref))):
        gate = _sigmoid(g_ref[0].astype(F32) + bg_ref[n:n + 1, :])
        term = gate * _dot(o_ref[0], wb_ref[n])
        acc = term if acc is None else acc + term
    out_ref[0] = acc.astype(BF16)


def _branch_mix(o_a, o_m, o_s, u, w_branch, b_gate3):
    g, m, _ = o_a.shape
    tm = min(1024, m)
    d = D_MODEL
    o_spec = pl.BlockSpec((1, tm, BRANCH_W), lambda b, i, j: (b, i, 0))

    def gate_spec(n):
        base = (U_GATE + n * d) // MIX_TN
        return pl.BlockSpec((1, tm, MIX_TN), lambda b, i, j: (b, i, base + j))

    return pl.pallas_call(
        _branch_mix_kernel,
        grid=(g, m // tm, d // MIX_TN),
        in_specs=[o_spec, o_spec, o_spec,
                  pl.BlockSpec((3, BRANCH_W, MIX_TN), lambda b, i, j: (0, 0, j)),
                  gate_spec(0), gate_spec(1), gate_spec(2),
                  pl.BlockSpec((3, MIX_TN), lambda b, i, j: (0, j))],
        out_specs=pl.BlockSpec((1, tm, MIX_TN), lambda b, i, j: (b, i, j)),
        out_shape=jax.ShapeDtypeStruct((g, m, d), BF16),
        compiler_params=_cparams("arbitrary", "arbitrary", "arbitrary"),
        name="branch_mix",
    )(o_a, o_m, o_s, w_branch, u, u, u, b_gate3)


def _out_proj_kernel(mix_ref, w_ref, x_ref, mod_ref, g_ref, out_ref):
    y = _dot(mix_ref[0], w_ref[...])
    yn = y * lax.rsqrt(jnp.mean(y * y, axis=-1, keepdims=True) + EPS) * g_ref[...]
    out_ref[0] = x_ref[0] + mod_ref[0, 2:3, :] * yn


def _out_proj(mixed, w_out, x3, mod3, g_post):
    g, m, d = x3.shape
    tm = min(512, m)
    return pl.pallas_call(
        _out_proj_kernel,
        grid=(g, m // tm),
        in_specs=[pl.BlockSpec((1, tm, d), lambda b, i: (b, i, 0)),
                  pl.BlockSpec((d, d), lambda b, i: (0, 0)),
                  pl.BlockSpec((1, tm, d), lambda b, i: (b, i, 0)),
                  pl.BlockSpec((1, 3, d), lambda b, i: (b, 0, 0)),
                  pl.BlockSpec((1, d), lambda b, i: (0, 0))],
        out_specs=pl.BlockSpec((1, tm, d), lambda b, i: (b, i, 0)),
        out_shape=jax.ShapeDtypeStruct((g, m, d), F32),
        compiler_params=_cparams("arbitrary", "arbitrary"),
        name="out_proj",
    )(mixed, w_out, x3, mod3, g_post.reshape(1, d))


def _pack_in_weights(w_in, w_gate):
    o = 0
    cols = {}
    for name, w in (("att_q", BRANCH_W), ("att_k", ATT_KV_W), ("att_v", ATT_KV_W), ("att_z", BRANCH_W),
                    ("ml_q", BRANCH_W), ("ml_k", BRANCH_W), ("ml_v", BRANCH_W), ("ml_o", BRANCH_W),
                    ("ml_z", BRANCH_W), ("ml_gates", 4 * ML_HEADS), ("ssm_xbc", SSM_CONV_CH),
                    ("ssm_dt", 2 * SSM_HEADS), ("ssm_z", BRANCH_W)):
        cols[name] = w_in[:, o:o + w]
        o += w
    perm = np.concatenate([np.arange(0, ATT_HD, 2), np.arange(1, ATT_HD, 2)])
    perm_q = (np.arange(ATT_HEADS)[:, None] * ATT_HD + perm[None, :]).reshape(-1)
    perm_k = (np.arange(ATT_KV_HEADS)[:, None] * ATT_HD + perm[None, :]).reshape(-1)
    xbc = cols["ssm_xbc"]
    big = jnp.concatenate([
        cols["att_q"][:, perm_q], cols["att_z"], cols["ml_q"], cols["ml_k"] * (ML_HD ** -0.5),
        cols["ml_v"], cols["ml_o"], cols["ml_z"], cols["ssm_z"],
        xbc, cols["att_k"][:, perm_k], cols["att_v"], w_gate], axis=1).astype(BF16)
    small = jnp.concatenate([cols["ml_gates"], cols["ssm_dt"],
                             jnp.zeros((w_in.shape[0], S_COLS - 4 * ML_HEADS - 2 * SSM_HEADS), F32)],
                            axis=1).astype(BF16)
    return big, small


def _rope_tables(seq):
    rows = seq // GRID_W
    row = jnp.repeat(jnp.arange(rows), GRID_W).astype(F32)
    col = jnp.tile(jnp.arange(GRID_W), rows).astype(F32)
    n_freq = ATT_HD // 4
    inv = ROPE_BASE ** (-jnp.arange(n_freq, dtype=F32) / n_freq)
    ang = jnp.concatenate([row[:, None] * inv, col[:, None] * inv], -1)
    cos2 = jnp.concatenate([jnp.cos(ang), jnp.cos(ang)], -1)
    sin2 = jnp.concatenate([-jnp.sin(ang), jnp.sin(ang)], -1)
    return cos2 * ATT_SCALE, sin2 * ATT_SCALE, cos2, sin2


def _ssd_params(dt_bias, a_log, d_skip, norm_g):
    pad_l = S_SSM_DT
    pad_r = S_COLS - S_SSM_DT - 2 * SSM_HEADS
    dtb = jnp.pad(dt_bias.astype(F32).reshape(1, -1), ((0, 0), (pad_l, pad_r)))
    a_row = jnp.pad(-jnp.exp(a_log.astype(F32)).reshape(1, -1), ((0, 0), (pad_l, pad_r)))
    chan_head = np.arange(BRANCH_W) // SSM_HD
    e = np.zeros((2, S_COLS, BRANCH_W), np.float32)
    for d in range(2):
        e[d, S_SSM_DT + SSM_HEADS * d + chan_head, np.arange(BRANCH_W)] = 1.0
    even = ((chan_head % 2) == 0).astype(np.float32).reshape(1, BRANCH_W)
    dsk = jnp.repeat(d_skip.astype(F32), SSM_HD).reshape(1, BRANCH_W)
    return (dtb, a_row, jnp.asarray(e, BF16), jnp.asarray(even), dsk,
            norm_g.astype(F32).reshape(1, BRANCH_W))


def kernel(x, c, ctx, c_ctx, w_mod, b_mod, g_pre, w_in, w_gate, b_gate, att_sink, ml_gate_bias,
           ml_norm, ssm_conv_w, ssm_conv_b, ssm_dt_bias, ssm_a_log, ssm_d, ssm_norm, w_branch,
           w_out, g_post):
    b, s, d = x.shape
    lc = ctx.shape[1]
    depth = w_mod.shape[0]
    rows = -(-(b + 1) // 8) * 8
    cc = jnp.concatenate([c, c_ctx[None, :], jnp.zeros((rows - b - 1, d), F32)], axis=0)
    mod = _modulation(cc, w_mod, b_mod)
    tables = _rope_tables(s)
    xc = ctx.reshape(1, b * lc, d)
    for l in range(depth):
        need_ctx = l < depth - 1
        w_big, w_small = _pack_in_weights(w_in[l], w_gate[l])
        mod_x = mod[l, :b].reshape(b, 3, d)
        mod_c = mod[l, b:b + 1].reshape(1, 3, d)
        u, us = _in_proj(x, mod_x, g_pre[l], w_big, w_small)
        uc, usc = _in_proj(xc, mod_c, g_pre[l], w_big, w_small)
        uc = uc.reshape(b, lc, -1)
        usc = usc.reshape(b, lc, -1)

        sink = att_sink[l].astype(F32)
        o_a = _attention(u, uc, sink, tables)

        gbias = jnp.pad(ml_gate_bias[l].astype(F32).reshape(1, -1),
                        ((0, 0), (S_ML_GATES, S_COLS - S_ML_GATES - 4 * ML_HEADS)))
        st0 = (jnp.zeros((b, 2 * ML_HEADS, ML_HD, ML_HD), F32),
               jnp.zeros((b, 2 * ML_HEADS, ML_HD), F32),
               jnp.zeros((b, 2 * ML_HEADS, LANES), F32))
        o_m_c, st = _mlstm(uc, usc, gbias, ml_norm[l].astype(F32), st0)
        o_m, _ = _mlstm(u, us, gbias, ml_norm[l].astype(F32), st)

        sp = _ssd_params(ssm_dt_bias[l], ssm_a_log[l], ssm_d[l], ssm_norm[l])
        h0 = jnp.zeros((b, 2 * SSM_GROUPS, SSM_STATE, SSM_HPG * SSM_HD), F32)
        o_s_c, hst = _ssd(_ssm_conv(uc, ssm_conv_w[l], ssm_conv_b[l]), uc, usc, sp, h0)
        o_s, _ = _ssd(_ssm_conv(u, ssm_conv_w[l], ssm_conv_b[l]), u, us, sp, hst)

        wb = w_branch[l].astype(BF16)
        bg = b_gate[l].astype(F32).reshape(3, d)
        wo = w_out[l].astype(BF16)
        mixed = _branch_mix(o_a, o_m, o_s, u, wb, bg)
        x_new = _out_proj(mixed, wo, x, mod_x, g_post[l])
        if need_ctx:
            o_a_c = _ctx_attention(uc, sink)
            flat = lambda a: a.reshape(1, b * lc, a.shape[-1])
            mixed_c = _branch_mix(flat(o_a_c), flat(o_m_c), flat(o_s_c), flat(uc), wb, bg)
            xc = _out_proj(mixed_c, wo, xc, mod_c, g_post[l])
        x = x_new
    return x
```

```python
import functools

import numpy as np
import jax
import jax.numpy as jnp
from jax import lax
from jax.experimental import pallas as pl
from jax.experimental.pallas import tpu as pltpu

F32 = jnp.float32
BF16 = jnp.bfloat16
LOG2E = 1.4426950408889634

D_MODEL = 2048
EPS = 1e-6
BRANCH_W = D_MODEL // 2
GRID_W = 64
ROPE_BASE = 10000.0

ATT_HD = 128
ATT_HEADS = BRANCH_W // ATT_HD
ATT_KV_HEADS = ATT_HEADS // 4
ATT_REP = ATT_HEADS // ATT_KV_HEADS
ATT_KV_W = ATT_KV_HEADS * ATT_HD
ATT_BLOCK = 128
ATT_SCALE = ATT_HD ** -0.5

ML_HEADS = 4
ML_HD = BRANCH_W // ML_HEADS

SSM_HD = 64
SSM_HEADS = BRANCH_W // SSM_HD
SSM_GROUPS = 2
SSM_HPG = SSM_HEADS // SSM_GROUPS
SSM_STATE = 128
SSM_BC_W = SSM_GROUPS * SSM_STATE
SSM_CONV = 5
SSM_CONV_CH = BRANCH_W + 2 * SSM_BC_W

LANES = 128
CHUNK = LANES
NEG = -1e30
VMEM_LIMIT = 56 * 1024 * 1024

U_ML_K = 0
U_ML_V = 1024
U_SSM_X = 2048
U_SSM_B = 3072
U_SSM_C = 3328
U_ATT_K = 3584
U_ATT_V = 3840
U_ATT_Q = 4096
U_ATT_Z = 5120
U_ML_Q = 6144
U_ML_O = 7168
U_ML_Z = 8192
U_SSM_Z = 9216
U_GATE = 10240
U_COLS = U_GATE + 3 * D_MODEL
S_ML_I = 0
S_ML_F = LANES
S_SSM_DT = 2 * LANES
S_COLS = 3 * LANES
ML_CW = ML_HD + LANES

IN_TN = 1024
U_SILU_TILES = (U_ATT_Z // IN_TN, U_ML_Z // IN_TN, U_SSM_Z // IN_TN)
U_SIGMOID_TILE = U_ML_O // IN_TN
U_STATE_COLS = U_ATT_Q


def _cparams(*sem):
    return pltpu.CompilerParams(dimension_semantics=sem, vmem_limit_bytes=VMEM_LIMIT)


def _sigmoid(v):
    return 0.5 * jnp.tanh(0.5 * v) + 0.5


def _silu(v):
    return v * _sigmoid(v)


def _softplus(v):
    return jnp.maximum(v, 0.0) + jnp.log(1.0 + jnp.exp(-jnp.abs(v)))


def _log_sigmoid(v):
    return jnp.minimum(v, 0.0) - jnp.log(1.0 + jnp.exp(-jnp.abs(v)))


def _dot(a, b):
    return jnp.dot(a, b, preferred_element_type=F32)


def _dot_nt(a, b):
    return lax.dot_general(a, b, (((1,), (1,)), ((), ())), preferred_element_type=F32)


def _dot_tn(a, b):
    return lax.dot_general(a, b, (((0,), (0,)), ((), ())), preferred_element_type=F32)


def _scan_cumsum(tri_bf, v):
    hi = v.astype(BF16)
    r1 = v - hi.astype(F32)
    mid = r1.astype(BF16)
    lo = (r1 - mid.astype(F32)).astype(BF16)
    res = _dot(tri_bf, jnp.concatenate([hi, mid, lo], axis=1))
    return res[:, :LANES] + res[:, LANES:2 * LANES] + res[:, 2 * LANES:]


def _expand2(v, e_mat):
    hi = v.astype(BF16)
    lo = (v - hi.astype(F32)).astype(BF16)
    return _dot(hi, e_mat) + _dot(lo, e_mat)


def _mod_kernel(c_ref, w_ref, b_ref, o_ref):
    sc = _silu(c_ref[...]).astype(BF16)
    o_ref[0] = _dot(sc, w_ref[0].astype(BF16)) + b_ref[0]


def _modulation(cc, w_mod, b_mod):
    depth, d, n = w_mod.shape
    rows = cc.shape[0]
    tn = 1024
    return pl.pallas_call(
        _mod_kernel,
        grid=(depth, n // tn),
        in_specs=[pl.BlockSpec((rows, d), lambda l, j: (0, 0)),
                  pl.BlockSpec((1, d, tn), lambda l, j: (l, 0, j)),
                  pl.BlockSpec((1, 1, tn), lambda l, j: (l, 0, j))],
        out_specs=pl.BlockSpec((1, rows, tn), lambda l, j: (l, 0, j)),
        out_shape=jax.ShapeDtypeStruct((depth, rows, n), F32),
        compiler_params=_cparams("arbitrary", "arbitrary"),
        name="modulation",
    )(cc, w_mod, b_mod.reshape(depth, 1, n))


def _in_proj_kernel(x_ref, mod_ref, g_ref, w_ref, ws_ref, u_ref, us_ref, h_scr):
    j = pl.program_id(2)

    @pl.when(j == 0)
    def _():
        x = x_ref[0]
        ms = jnp.mean(x * x, axis=-1, keepdims=True)
        xn = x * lax.rsqrt(ms + EPS) * g_ref[...]
        h = xn * (1.0 + mod_ref[0, 1:2, :]) + mod_ref[0, 0:1, :]
        hb = h.astype(BF16)
        h_scr[...] = hb
        us_ref[0] = _dot(hb, ws_ref[...])

    is_silu = functools.reduce(jnp.logical_or, [j == t for t in U_SILU_TILES])
    is_sigmoid = j == U_SIGMOID_TILE

    @pl.when(is_silu)
    def _():
        u_ref[0] = _silu(_dot(h_scr[...], w_ref[...])).astype(BF16)

    @pl.when(is_sigmoid)
    def _():
        u_ref[0] = _sigmoid(_dot(h_scr[...], w_ref[...])).astype(BF16)

    @pl.when(jnp.logical_not(jnp.logical_or(is_silu, is_sigmoid)))
    def _():
        u_ref[0] = _dot(h_scr[...], w_ref[...]).astype(BF16)


def _in_proj(x3, mod3, g_pre, w_big, w_small, n=U_COLS):
    g, m, d = x3.shape
    tm = min(1024, m)
    return pl.pallas_call(
        _in_proj_kernel,
        grid=(g, m // tm, n // IN_TN),
        in_specs=[pl.BlockSpec((1, tm, d), lambda b, i, j: (b, i, 0)),
                  pl.BlockSpec((1, 3, d), lambda b, i, j: (b, 0, 0)),
                  pl.BlockSpec((1, d), lambda b, i, j: (0, 0)),
                  pl.BlockSpec((d, IN_TN), lambda b, i, j: (0, j)),
                  pl.BlockSpec((d, S_COLS), lambda b, i, j: (0, 0))],
        out_specs=[pl.BlockSpec((1, tm, IN_TN), lambda b, i, j: (b, i, j)),
                   pl.BlockSpec((1, tm, S_COLS), lambda b, i, j: (b, i, 0))],
        out_shape=[jax.ShapeDtypeStruct((g, m, n), BF16),
                   jax.ShapeDtypeStruct((g, m, S_COLS), F32)],
        scratch_shapes=[pltpu.VMEM((tm, d), BF16)],
        compiler_params=_cparams("arbitrary", "arbitrary", "arbitrary"),
        name="in_proj",
    )(x3, mod3, g_pre.reshape(1, d), w_big, w_small)


def _att_kernel(sink_ref, q_ref, z_ref, kp_ref, kc_ref, kn_ref, vp_ref, vc_ref, vn_ref,
                ctxk_ref, ctxv_ref, cq_ref, sq_ref, ck_ref, sk_ref, o_ref, *, nb, lc):
    i = pl.program_id(1)
    blk = ATT_BLOCK
    r0 = pl.multiple_of(i * blk, blk)
    rp = pl.multiple_of(jnp.maximum(i - 1, 0) * blk, blk)
    rn = pl.multiple_of(jnp.minimum(i + 1, nb - 1) * blk, blk)

    def rope(t, cos2, sin2):
        return t * cos2 + pltpu.roll(t, ATT_HD // 2, 1) * sin2

    cq = cq_ref[pl.ds(r0, blk), :]
    sq = sq_ref[pl.ds(r0, blk), :]
    key = lax.broadcasted_iota(jnp.int32, (blk, blk), 0)
    qry = lax.broadcasted_iota(jnp.int32, (blk, blk), 1)
    bias_p = jnp.where(key >= qry, 0.0, NEG) + jnp.where(i > 0, 0.0, NEG)
    bias_n = jnp.where(key <= qry, 0.0, NEG) + jnp.where(i < nb - 1, 0.0, NEG)
    bias_p = jnp.concatenate([bias_p, bias_p], axis=1)
    bias_n = jnp.concatenate([bias_n, bias_n], axis=1)
    scores = []
    for g in range(ATT_KV_HEADS):
        ks = slice(g * ATT_HD, (g + 1) * ATT_HD)
        kparts = []
        for k_ref, r in ((kp_ref, rp), (kc_ref, r0), (kn_ref, rn)):
            kr = rope(k_ref[0][:, ks].astype(F32), ck_ref[pl.ds(r, blk), :], sk_ref[pl.ds(r, blk), :])
            kparts.append(kr.astype(BF16))
        kk = jnp.concatenate(kparts + [ctxk_ref[0][:, ks]], axis=0)
        vv = jnp.concatenate([vp_ref[0][:, ks], vc_ref[0][:, ks], vn_ref[0][:, ks],
                              ctxv_ref[0][:, ks]], axis=0)
        vv_t = vv.T
        for pair in range(ATT_REP // 2):
            heads = [g * ATT_REP + 2 * pair + j for j in range(2)]
            q2 = jnp.concatenate(
                [rope(q_ref[0][:, hh * ATT_HD:(hh + 1) * ATT_HD].astype(F32), cq, sq).astype(BF16)
                 for hh in heads], axis=0)
            scores.append((heads, vv_t, _dot_nt(kk, q2)))
    for heads, vv_t, s_t in scores:
        sink = jnp.concatenate([jnp.full((1, blk), sink_ref[hh] * LOG2E, F32) for hh in heads],
                               axis=1)
        s_t = jnp.concatenate([s_t[:blk] + bias_p, s_t[blk:2 * blk],
                               s_t[2 * blk:3 * blk] + bias_n, s_t[3 * blk:]], axis=0)
        m = jnp.maximum(jnp.max(s_t, axis=0, keepdims=True), sink)
        e_t = jnp.exp2(s_t - m)
        den = jnp.sum(e_t, axis=0, keepdims=True) + jnp.exp2(sink - m)
        o_t = _dot(vv_t, e_t.astype(BF16)) / den
        for j, hh in enumerate(heads):
            hs = slice(hh * ATT_HD, (hh + 1) * ATT_HD)
            o = o_t[:, j * blk:(j + 1) * blk].T
            o_ref[0, :, hs] = (o * z_ref[0][:, hs].astype(F32)).astype(BF16)


def _attention(u, uc, sink, tables):
    b, s, _ = u.shape
    lc = uc.shape[1]
    nb = s // ATT_BLOCK
    blk = ATT_BLOCK
    kcol = U_ATT_K // ATT_KV_W
    vcol = U_ATT_V // ATT_KV_W

    def kv_spec(col, shift):
        return pl.BlockSpec((1, blk, ATT_KV_W),
                            lambda bb, i: (bb, jnp.clip(i + shift, 0, nb - 1), col))

    tab_spec = pl.BlockSpec((s, ATT_HD), lambda bb, i: (0, 0))
    return pl.pallas_call(
        functools.partial(_att_kernel, nb=nb, lc=lc),
        grid=(b, nb),
        in_specs=[pl.BlockSpec(memory_space=pltpu.SMEM),
                  pl.BlockSpec((1, blk, BRANCH_W), lambda bb, i: (bb, i, U_ATT_Q // BRANCH_W)),
                  pl.BlockSpec((1, blk, BRANCH_W), lambda bb, i: (bb, i, U_ATT_Z // BRANCH_W)),
                  kv_spec(kcol, -1), kv_spec(kcol, 0), kv_spec(kcol, 1),
                  kv_spec(vcol, -1), kv_spec(vcol, 0), kv_spec(vcol, 1),
                  pl.BlockSpec((1, lc, ATT_KV_W), lambda bb, i: (bb, 0, kcol)),
                  pl.BlockSpec((1, lc, ATT_KV_W), lambda bb, i: (bb, 0, vcol)),
                  tab_spec, tab_spec, tab_spec, tab_spec],
        out_specs=pl.BlockSpec((1, blk, BRANCH_W), lambda bb, i: (bb, i, 0)),
        out_shape=jax.ShapeDtypeStruct((b, s, BRANCH_W), BF16),
        compiler_params=_cparams("arbitrary", "arbitrary"),
        name="window_attention",
    )(sink, u, u, u, u, u, u, u, u, uc, uc, *tables)


def _ctx_att_kernel(sink_ref, q_ref, z_ref, k_ref, v_ref, o_ref):
    for hh in range(ATT_HEADS):
        g = hh // ATT_REP
        ks = slice(g * ATT_HD, (g + 1) * ATT_HD)
        hs = slice(hh * ATT_HD, (hh + 1) * ATT_HD)
        qh = (q_ref[0][:, hs].astype(F32) * ATT_SCALE).astype(BF16)
        sink = sink_ref[hh]
        s = _dot_nt(qh, k_ref[0][:, ks])
        m = jnp.maximum(jnp.max(s, axis=1, keepdims=True), sink)
        e = jnp.exp(s - m)
        den = jnp.sum(e, axis=1, keepdims=True) + jnp.exp(sink - m)
        o = _dot(e.astype(BF16), v_ref[0][:, ks]) / den
        o_ref[0, :, hs] = (o * z_ref[0][:, hs].astype(F32)).astype(BF16)


def _ctx_attention(uc, sink):
    b, lc, _ = uc.shape
    kcol = U_ATT_K // ATT_KV_W
    vcol = U_ATT_V // ATT_KV_W
    return pl.pallas_call(
        _ctx_att_kernel,
        grid=(b,),
        in_specs=[pl.BlockSpec(memory_space=pltpu.SMEM),
                  pl.BlockSpec((1, lc, BRANCH_W), lambda bb: (bb, 0, U_ATT_Q // BRANCH_W)),
                  pl.BlockSpec((1, lc, BRANCH_W), lambda bb: (bb, 0, U_ATT_Z // BRANCH_W)),
                  pl.BlockSpec((1, lc, ATT_KV_W), lambda bb: (bb, 0, kcol)),
                  pl.BlockSpec((1, lc, ATT_KV_W), lambda bb: (bb, 0, vcol))],
        out_specs=pl.BlockSpec((1, lc, BRANCH_W), lambda bb: (bb, 0, 0)),
        out_shape=jax.ShapeDtypeStruct((b, lc, BRANCH_W), BF16),
        compiler_params=_cparams("arbitrary"),
        name="context_attention",
    )(sink, uc, uc, uc, uc)


def _scan_masks(reverse, n):
    row = lax.broadcasted_iota(jnp.int32, (n, n), 0)
    col = lax.broadcasted_iota(jnp.int32, (n, n), 1)
    tri = (col >= row) if reverse else (col <= row)
    return tri, jnp.where(tri, 1.0, 0.0).astype(BF16)


def _cummax_rows(a, reverse):
    n = a.shape[0]
    row = lax.broadcasted_iota(jnp.int32, a.shape, 0)
    k = 1
    while k < n:
        if reverse:
            a = jnp.maximum(a, jnp.where(row < n - k, pltpu.roll(a, n - k, 0), NEG))
        else:
            a = jnp.maximum(a, jnp.where(row >= k, pltpu.roll(a, k, 0), NEG))
        k *= 2
    return a


def _run_skewed(gens):
    pending = list(gens)
    active = []
    while pending or active:
        if pending:
            active.append(pending.pop(0))
        for gen in list(active):
            if next(gen, StopIteration) is StopIteration:
                active.remove(gen)


def _mlstm_gates(reverse, g_ref, gbias_ref, m_row):
    n = CHUNK
    last = 0 if reverse else n - 1
    tri, trif = _scan_masks(reverse, n)
    g_all = g_ref[0] + gbias_ref[...]
    b_all = _scan_cumsum(trif, _log_sigmoid(g_all[:, S_ML_F:S_ML_F + LANES]))
    a_all = g_all[:, S_ML_I:S_ML_I + LANES] - b_all
    mx_all = jnp.maximum(_cummax_rows(a_all, reverse), m_row)
    mt_all = b_all + mx_all
    return dict(tri=tri, mx=mx_all, fl=jnp.exp(-mt_all), mx_last=mx_all[last:last + 1, :],
                a_t=a_all.T, m_new=mt_all[last:last + 1, :])


def _mlstm_head(gates, ln, hs, q_ref, k_ref, v_ref, c_ref, m_row, outs):
    n = CHUNK
    a_row = gates["a_t"][ln:ln + 1, :]
    m_prev = m_row[:, ln:ln + 1]
    mxl = gates["mx_last"][:, ln:ln + 1]
    k_t = k_ref[0][:, hs].T
    v1 = jnp.concatenate([v_ref[0][:, hs], jnp.ones((n, LANES), BF16)], axis=1)
    c_prev = c_ref[0, ln]
    if q_ref is not None:
        qh = q_ref[0][:, hs]
        s = _dot(qh, k_t)
        qc = _dot(qh, c_prev.astype(BF16))
    yield
    kw_t = (k_t.astype(F32) * jnp.exp(a_row - mxl)).astype(BF16)
    upd = _dot(kw_t, v1)
    yield
    if q_ref is not None:
        mx_b = jnp.broadcast_to(gates["mx"][:, ln:ln + 1], (n, LANES))
        w = jnp.exp(jnp.where(gates["tri"], a_row - mx_b, NEG))
        pv = _dot((s * w).astype(BF16), v1)
        yield
    c_ref[0, ln] = jnp.exp(m_prev - mxl) * c_prev + upd
    if q_ref is not None:
        yield
        fl_b = jnp.broadcast_to(gates["fl"][:, ln:ln + 1], (n, LANES))
        e_int = jnp.exp(m_prev - mx_b)
        nd = jnp.concatenate([e_int] * (ML_CW // LANES), axis=1) * qc + pv
        r = 1.0 / jnp.maximum(jnp.abs(nd[:, ML_HD:]), fl_b)
        outs[ln] = nd[:, :ML_HD] * jnp.concatenate([r] * (ML_HD // LANES), axis=1)


def _mlstm_step(fwd_refs, bwd_refs, gbias_ref, c_ref, m_row):
    gates = [_mlstm_gates(bool(d), refs[3], gbias_ref, m_row)
             for d, refs in enumerate((fwd_refs, bwd_refs))]
    outs = {}
    gens = []
    for h in range(ML_HEADS):
        for d, refs in enumerate((fwd_refs, bwd_refs)):
            gens.append(_mlstm_head(gates[d], d * ML_HEADS + h, slice(h * ML_HD, (h + 1) * ML_HD),
                                    refs[0], refs[1], refs[2], c_ref, m_row, outs))
    _run_skewed(gens)
    lane = lax.broadcasted_iota(jnp.int32, (1, LANES), 1)
    m_new = jnp.where(lane < ML_HEADS, gates[0]["m_new"], gates[1]["m_new"])
    if fwd_refs[0] is None:
        return None, None, m_new
    h_f = jnp.concatenate([outs[h] for h in range(ML_HEADS)], axis=1)
    h_b = jnp.concatenate([outs[ML_HEADS + h] for h in range(ML_HEADS)], axis=1)
    return h_f, h_b, m_new


def _mlstm_finish(h_sum, o_ref, z_ref, ng_ref):
    t = o_ref[0].astype(F32) * h_sum
    parts = []
    for h in range(ML_HEADS):
        th = t[:, h * ML_HD:(h + 1) * ML_HD]
        parts.append(th * lax.rsqrt(jnp.mean(th * th, axis=1, keepdims=True) + EPS))
    hn = jnp.concatenate(parts, axis=1)
    return (hn * ng_ref[...] * z_ref[0].astype(F32)).astype(BF16)


def _mlstm_kernel(qf, kf, vf, gf, of, zf, qb, kb, vb, gb, ob, zb, gbias, ng, c0, m0,
                  out, cfin, mfin, h_scr, *, nc):
    s = pl.program_id(1)

    @pl.when(s == 0)
    def _():
        cfin[...] = c0[...]
        mfin[...] = m0[...]

    m_row = mfin[0, 0:1, :]
    h_f, h_b, m_new = _mlstm_step((qf, kf, vf, gf), (qb, kb, vb, gb), gbias, cfin, m_row)
    mfin[0] = jnp.broadcast_to(m_new, (8, LANES))
    rf = pl.multiple_of(s * CHUNK, CHUNK)
    rb = pl.multiple_of((nc - 1 - s) * CHUNK, CHUNK)

    @pl.when(s < nc // 2)
    def _():
        h_scr[pl.ds(rf, CHUNK), :] = h_f
        h_scr[pl.ds(rb, CHUNK), :] = h_b

    @pl.when(s >= nc // 2)
    def _():
        out[0, pl.ds(rf, CHUNK), :] = _mlstm_finish(h_f + h_scr[pl.ds(rf, CHUNK), :], of, zf, ng)
        out[0, pl.ds(rb, CHUNK), :] = _mlstm_finish(h_b + h_scr[pl.ds(rb, CHUNK), :], ob, zb, ng)


def _mlstm_state_kernel(kf, vf, gf, kb, vb, gb, gbias, c0, m0, cfin, mfin):
    @pl.when(pl.program_id(1) == 0)
    def _():
        cfin[...] = c0[...]
        mfin[...] = m0[...]

    m_row = mfin[0, 0:1, :]
    _, _, m_new = _mlstm_step((None, kf, vf, gf), (None, kb, vb, gb), gbias, cfin, m_row)
    mfin[0] = jnp.broadcast_to(m_new, (8, LANES))


def _mlstm_state(u, us, gbias_row, state):
    b, t, _ = u.shape
    nc = t // CHUNK
    c0, m0 = state
    gw = 2 * LANES

    def chunk(width, col, rev):
        if rev:
            return pl.BlockSpec((1, CHUNK, width), lambda bb, s: (bb, nc - 1 - s, col))
        return pl.BlockSpec((1, CHUNK, width), lambda bb, s: (bb, s, col))

    def dir_specs(rev):
        return [chunk(BRANCH_W, U_ML_K // BRANCH_W, rev), chunk(BRANCH_W, U_ML_V // BRANCH_W, rev),
                chunk(gw, 0, rev)]

    c_spec = pl.BlockSpec((1, 2 * ML_HEADS, ML_HD, ML_CW), lambda bb, s: (bb, 0, 0, 0))
    m_spec = pl.BlockSpec((1, 8, LANES), lambda bb, s: (bb, 0, 0))
    cf, mf = pl.pallas_call(
        _mlstm_state_kernel,
        grid=(b, nc),
        in_specs=dir_specs(False) + dir_specs(True) + [
            pl.BlockSpec((1, gw), lambda bb, s: (0, 0)), c_spec, m_spec],
        out_specs=[c_spec, m_spec],
        out_shape=[jax.ShapeDtypeStruct(c0.shape, F32), jax.ShapeDtypeStruct(m0.shape, F32)],
        compiler_params=_cparams("arbitrary", "arbitrary"),
        name="mlstm_state",
    )(u, u, us, u, u, us, gbias_row, c0, m0)
    return cf, mf


def _mlstm(u, us, gbias_row, norm_g, state):
    b, t, _ = u.shape
    nc = t // CHUNK
    assert nc % 2 == 0
    c0, m0 = state
    nst = 2 * ML_HEADS
    gw = 2 * LANES

    def fwd(col):
        return pl.BlockSpec((1, CHUNK, BRANCH_W), lambda bb, s: (bb, s, col // BRANCH_W))

    def bwd(col):
        return pl.BlockSpec((1, CHUNK, BRANCH_W), lambda bb, s: (bb, nc - 1 - s, col // BRANCH_W))

    gate_f = pl.BlockSpec((1, CHUNK, gw), lambda bb, s: (bb, s, 0))
    gate_b = pl.BlockSpec((1, CHUNK, gw), lambda bb, s: (bb, nc - 1 - s, 0))
    c_spec = pl.BlockSpec((1, nst, ML_HD, ML_CW), lambda bb, s: (bb, 0, 0, 0))
    m_spec = pl.BlockSpec((1, 8, LANES), lambda bb, s: (bb, 0, 0))
    out, cf, mf = pl.pallas_call(
        functools.partial(_mlstm_kernel, nc=nc),
        grid=(b, nc),
        in_specs=[fwd(U_ML_Q), fwd(U_ML_K), fwd(U_ML_V), gate_f, fwd(U_ML_O), fwd(U_ML_Z),
                  bwd(U_ML_Q), bwd(U_ML_K), bwd(U_ML_V), gate_b, bwd(U_ML_O), bwd(U_ML_Z),
                  pl.BlockSpec((1, gw), lambda bb, s: (0, 0)),
                  pl.BlockSpec((1, BRANCH_W), lambda bb, s: (0, 0)),
                  c_spec, m_spec],
        out_specs=[pl.BlockSpec((1, t, BRANCH_W), lambda bb, s: (bb, 0, 0)), c_spec, m_spec],
        out_shape=[jax.ShapeDtypeStruct((b, t, BRANCH_W), BF16),
                   jax.ShapeDtypeStruct(c0.shape, F32),
                   jax.ShapeDtypeStruct(m0.shape, F32)],
        scratch_shapes=[pltpu.VMEM((t, BRANCH_W), F32)],
        compiler_params=_cparams("arbitrary", "arbitrary"),
        name="mlstm",
    )(u, u, u, us, u, u, u, u, u, us, u, u, gbias_row, norm_g.reshape(1, BRANCH_W), c0, m0)
    return out, (cf, mf)


CONV_TN = 512
CONV_PAD = 16


def _conv_kernel(x_ref, w_ref, b_ref, o_ref, pad_scr, *, t):
    pad_scr[0:CONV_PAD, :] = jnp.zeros((CONV_PAD, CONV_TN), F32)
    pad_scr[CONV_PAD + t:2 * CONV_PAD + t, :] = jnp.zeros((CONV_PAD, CONV_TN), F32)
    pad_scr[CONV_PAD:CONV_PAD + t, :] = x_ref[0].astype(F32)
    rows = 256
    half = SSM_CONV // 2
    for c in range(t // rows):
        r0 = c * rows
        acc = jnp.broadcast_to(b_ref[...], (rows, CONV_TN))
        for k in range(SSM_CONV):
            lo = CONV_PAD + r0 + k - half
            acc = acc + w_ref[k:k + 1, :] * pad_scr[lo:lo + rows, :]
        o_ref[0, r0:r0 + rows, :] = _silu(acc).astype(BF16)


def _ssm_conv(u, conv_w, conv_b):
    b, t, _ = u.shape
    w8 = jnp.zeros((8, SSM_CONV_CH), F32).at[:SSM_CONV].set(conv_w)
    first = U_SSM_X // CONV_TN
    return pl.pallas_call(
        functools.partial(_conv_kernel, t=t),
        grid=(b, SSM_CONV_CH // CONV_TN),
        in_specs=[pl.BlockSpec((1, t, CONV_TN), lambda bb, j: (bb, 0, first + j)),
                  pl.BlockSpec((8, CONV_TN), lambda bb, j: (0, j)),
                  pl.BlockSpec((1, CONV_TN), lambda bb, j: (0, j))],
        out_specs=pl.BlockSpec((1, t, CONV_TN), lambda bb, j: (bb, 0, j)),
        out_shape=jax.ShapeDtypeStruct((b, t, SSM_CONV_CH), BF16),
        scratch_shapes=[pltpu.VMEM((t + 2 * CONV_PAD, CONV_TN), F32)],
        compiler_params=_cparams("arbitrary", "arbitrary"),
        name="ssm_conv",
    )(u, w8, conv_b.reshape(1, SSM_CONV_CH))


def _ssd_dir(reverse, x_ref, bm_ref, cm_ref, dt_ref, dtb_ref, a_ref, e_ref, even_ref, h_ref):
    d = 1 if reverse else 0
    n = CHUNK
    last = 0 if reverse else n - 1
    tri, trif = _scan_masks(reverse, n)
    dt_all = _softplus(dt_ref[0] + dtb_ref[...])
    cum = _scan_cumsum(trif, dt_all * a_ref[...])
    total = cum[last:last + 1, :]
    cum2 = cum * LOG2E
    src2_t = (cum2 - jnp.log2(dt_all)).T
    e_mat = e_ref[d]
    wd_x = _dot((jnp.exp(total - cum) * dt_all).astype(BF16), e_mat)
    tot_x = _expand2(jnp.broadcast_to(jnp.exp(total), (8, LANES)), e_mat)[0:1, :]

    x = x_ref[0]
    pro = dict(tri=tri, cum2=cum2, src2_t=src2_t, x=x, d=d, e_mat=e_mat, tot_x=tot_x,
               xw=(x.astype(F32) * wd_x).astype(BF16), bm_ref=bm_ref, cm_ref=cm_ref, h_ref=h_ref)
    if cm_ref is not None:
        even = even_ref[...].astype(BF16)
        pro["x_par"] = (x * even, x * (1.0 - even))
        pro["dec"] = jnp.exp(cum)
        pro["cb"] = [_dot_nt(cm_ref[0][:, g * SSM_STATE:(g + 1) * SSM_STATE],
                             bm_ref[0][:, g * SSM_STATE:(g + 1) * SSM_STATE])
                     for g in range(SSM_GROUPS)]
    return pro


def _ssd_head(pro, g, p, par, terms):
    n = CHUNK
    ln = SSM_HEADS * pro["d"] + g * SSM_HPG + 2 * p + par
    lo = (g * SSM_HPG + 2 * p) * SSM_HD
    c_b = jnp.broadcast_to(pro["cum2"][:, ln:ln + 1], (n, LANES))
    seg = jnp.where(pro["tri"], c_b - pro["src2_t"][ln:ln + 1, :], NEG)
    mix = (pro["cb"][g] * jnp.exp2(seg)).astype(BF16)
    yield
    term = _dot(mix, pro["x_par"][par][:, lo:lo + 2 * SSM_HD])
    key = (pro["d"], g, p)
    terms[key] = term if par == 0 else terms[key] + term


def _ssd_group_end(pro, g, terms, ys):
    gw = SSM_HPG * SSM_HD
    gs = slice(g * gw, (g + 1) * gw)
    d = pro["d"]
    ss = slice(g * SSM_STATE, (g + 1) * SSM_STATE)
    h_prev = pro["h_ref"][0, d * SSM_GROUPS + g]
    if pro["cm_ref"] is not None:
        inter = _dot(pro["cm_ref"][0][:, ss], h_prev.astype(BF16))
        dec_x = _expand2(pro["dec"], pro["e_mat"][:, gs])
        ys[(d, g)] = (jnp.concatenate([terms[(d, g, p)] for p in range(SSM_HPG // 2)], axis=1)
                      + dec_x * inter)
    pro["h_ref"][0, d * SSM_GROUPS + g] = (pro["tot_x"][:, gs] * h_prev
                                           + _dot_tn(pro["bm_ref"][0][:, ss], pro["xw"][:, gs]))
    yield


def _ssd_step(fwd_refs, bwd_refs, dtb_ref, a_ref, e_ref, even_ref, h_ref):
    pros = [_ssd_dir(bool(d), refs[0], refs[1], refs[2], refs[3], dtb_ref, a_ref, e_ref, even_ref,
                     h_ref) for d, refs in enumerate((fwd_refs, bwd_refs))]
    with_out = fwd_refs[2] is not None
    terms, ys, gens = {}, {}, []
    for g in range(SSM_GROUPS):
        if with_out:
            gens += [_ssd_head(pros[d], g, p, par, terms)
                     for p in range(SSM_HPG // 2) for par in range(2) for d in range(2)]
        gens += [_ssd_group_end(pros[d], g, terms, ys) for d in range(2)]
    _run_skewed(gens)
    if not with_out:
        return None
    y_f, y_b = (jnp.concatenate([ys[(d, g)] for g in range(SSM_GROUPS)], axis=1) for d in range(2))
    return y_f, pros[0]["x"], y_b, pros[1]["x"]


def _ssd_finish(y_sum, x, z_ref, dsk_ref, ng_ref):
    y = (y_sum + dsk_ref[...] * x.astype(F32)) * z_ref[0].astype(F32)
    y = y * lax.rsqrt(jnp.mean(y * y, axis=1, keepdims=True) + EPS)
    return (y * ng_ref[...]).astype(BF16)


def _ssd_kernel(xf, bf, cf, sf, zf, xb, bb_, cb_, sb, zb, dtb, arow, emat, even, dsk, ng, h0,
                out, hfin, y_scr, *, nc):
    s = pl.program_id(1)

    @pl.when(s == 0)
    def _():
        hfin[...] = h0[...]

    y_f, x_f, y_b, x_b = _ssd_step((xf, bf, cf, sf), (xb, bb_, cb_, sb), dtb, arow, emat, even, hfin)
    rf = pl.multiple_of(s * CHUNK, CHUNK)
    rb = pl.multiple_of((nc - 1 - s) * CHUNK, CHUNK)

    @pl.when(s < nc // 2)
    def _():
        y_scr[pl.ds(rf, CHUNK), :] = y_f
        y_scr[pl.ds(rb, CHUNK), :] = y_b

    @pl.when(s >= nc // 2)
    def _():
        out[0, pl.ds(rf, CHUNK), :] = _ssd_finish(y_f + y_scr[pl.ds(rf, CHUNK), :], x_f, zf, dsk, ng)
        out[0, pl.ds(rb, CHUNK), :] = _ssd_finish(y_b + y_scr[pl.ds(rb, CHUNK), :], x_b, zb, dsk, ng)


def _ssd_state_kernel(xf, bf, sf, xb, bb_, sb, dtb, arow, emat, h0, hfin):
    @pl.when(pl.program_id(1) == 0)
    def _():
        hfin[...] = h0[...]

    _ssd_step((xf, bf, None, sf), (xb, bb_, None, sb), dtb, arow, emat, None, hfin)


def _ssd_state(xbc, us, params, h0):
    b, t, _ = xbc.shape
    nc = t // CHUNK
    dtb_row, a_row, e_mat = params[:3]

    def chunk(width, col, rev):
        if rev:
            return pl.BlockSpec((1, CHUNK, width), lambda bb, s: (bb, nc - 1 - s, col))
        return pl.BlockSpec((1, CHUNK, width), lambda bb, s: (bb, s, col))

    def dir_specs(rev):
        return [chunk(BRANCH_W, 0, rev), chunk(SSM_BC_W, BRANCH_W // SSM_BC_W, rev),
                chunk(LANES, S_SSM_DT // LANES, rev)]

    row_spec = pl.BlockSpec((1, LANES), lambda bb, s: (0, 0))
    h_spec = pl.BlockSpec((1, 2 * SSM_GROUPS, SSM_STATE, SSM_HPG * SSM_HD),
                          lambda bb, s: (bb, 0, 0, 0))
    return pl.pallas_call(
        _ssd_state_kernel,
        grid=(b, nc),
        in_specs=dir_specs(False) + dir_specs(True) + [
            row_spec, row_spec, pl.BlockSpec((2, LANES, BRANCH_W), lambda bb, s: (0, 0, 0)), h_spec],
        out_specs=h_spec,
        out_shape=jax.ShapeDtypeStruct(h0.shape, F32),
        compiler_params=_cparams("arbitrary", "arbitrary"),
        name="ssd_state",
    )(xbc, xbc, us, xbc, xbc, us, dtb_row, a_row, e_mat, h0)


def _ssd(xbc, u, us, params, h0):
    b, t, _ = u.shape
    nc = t // CHUNK
    assert nc % 2 == 0
    dtb_row, a_row, e_mat, even_row, dsk_row, ng_row = params
    nst = 2 * SSM_GROUPS
    gw = SSM_HPG * SSM_HD

    def chunk_spec(width, col, rev):
        if rev:
            return pl.BlockSpec((1, CHUNK, width), lambda bb, s: (bb, nc - 1 - s, col))
        return pl.BlockSpec((1, CHUNK, width), lambda bb, s: (bb, s, col))

    def dir_specs(rev):
        return [chunk_spec(BRANCH_W, 0, rev),
                chunk_spec(SSM_BC_W, BRANCH_W // SSM_BC_W, rev),
                chunk_spec(SSM_BC_W, BRANCH_W // SSM_BC_W + 1, rev),
                chunk_spec(LANES, S_SSM_DT // LANES, rev),
                chunk_spec(BRANCH_W, U_SSM_Z // BRANCH_W, rev)]

    def row_spec(width):
        return pl.BlockSpec((1, width), lambda bb, s: (0, 0))

    h_spec = pl.BlockSpec((1, nst, SSM_STATE, gw), lambda bb, s: (bb, 0, 0, 0))
    out, hf = pl.pallas_call(
        functools.partial(_ssd_kernel, nc=nc),
        grid=(b, nc),
        in_specs=dir_specs(False) + dir_specs(True) + [
            row_spec(LANES), row_spec(LANES),
            pl.BlockSpec((2, LANES, BRANCH_W), lambda bb, s: (0, 0, 0)),
            row_spec(BRANCH_W), row_spec(BRANCH_W), row_spec(BRANCH_W), h_spec],
        out_specs=[pl.BlockSpec((1, t, BRANCH_W), lambda bb, s: (bb, 0, 0)), h_spec],
        out_shape=[jax.ShapeDtypeStruct((b, t, BRANCH_W), BF16),
                   jax.ShapeDtypeStruct(h0.shape, F32)],
        scratch_shapes=[pltpu.VMEM((t, BRANCH_W), F32)],
        compiler_params=_cparams("arbitrary", "arbitrary"),
        name="ssd",
    )(xbc, xbc, xbc, us, u, xbc, xbc, xbc, us, u,
      dtb_row, a_row, e_mat, even_row, dsk_row, ng_row, h0)
    return out, hf


MIX_TN = 1024


def _branch_mix_kernel(oa_ref, om_ref, os_ref, wb_ref, ga_ref, gm_ref, gs_ref, bg_ref, out_ref):
    acc = None
    for n, (o_ref, g_ref) in enumerate(((oa_ref, ga_ref), (om_ref, gm_ref), (os_ref, gs_ref))):
        gate = _sigmoid(g_ref[0].astype(F32) + bg_ref[n:n + 1, :])
        term = gate * _dot(o_ref[0], wb_ref[n])
        acc = term if acc is None else acc + term
    out_ref[0] = acc.astype(BF16)


def _branch_mix(o_a, o_m, o_s, u, w_branch, b_gate3):
    g, m, _ = o_a.shape
    tm = min(1024, m)
    d = D_MODEL
    o_spec = pl.BlockSpec((1, tm, BRANCH_W), lambda b, i, j: (b, i, 0))

    def gate_spec(n):
        base = (U_GATE + n * d) // MIX_TN
        return pl.BlockSpec((1, tm, MIX_TN), lambda b, i, j: (b, i, base + j))

    return pl.pallas_call(
        _branch_mix_kernel,
        grid=(g, m // tm, d // MIX_TN),
        in_specs=[o_spec, o_spec, o_spec,
                  pl.BlockSpec((3, BRANCH_W, MIX_TN), lambda b, i, j: (0, 0, j)),
                  gate_spec(0), gate_spec(1), gate_spec(2),
                  pl.BlockSpec((3, MIX_TN), lambda b, i, j: (0, j))],
        out_specs=pl.BlockSpec((1, tm, MIX_TN), lambda b, i, j: (b, i, j)),
        out_shape=jax.ShapeDtypeStruct((g, m, d), BF16),
        compiler_params=_cparams("arbitrary", "arbitrary", "arbitrary"),
        name="branch_mix",
    )(o_a, o_m, o_s, w_branch, u, u, u, b_gate3)


def _out_proj_kernel(mix_ref, w_ref, x_ref, mod_ref, g_ref, out_ref):
    y = _dot(mix_ref[0], w_ref[...])
    yn = y * lax.rsqrt(jnp.mean(y * y, axis=-1, keepdims=True) + EPS) * g_ref[...]
    out_ref[0] = x_ref[0] + mod_ref[0, 2:3, :] * yn


def _out_proj(mixed, w_out, x3, mod3, g_post):
    g, m, d = x3.shape
    tm = min(512, m)
    return pl.pallas_call(
        _out_proj_kernel,
        grid=(g, m // tm),
        in_specs=[pl.BlockSpec((1, tm, d), lambda b, i: (b, i, 0)),
                  pl.BlockSpec((d, d), lambda b, i: (0, 0)),
                  pl.BlockSpec((1, tm, d), lambda b, i: (b, i, 0)),
                  pl.BlockSpec((1, 3, d), lambda b, i: (b, 0, 0)),
                  pl.BlockSpec((1, d), lambda b, i: (0, 0))],
        out_specs=pl.BlockSpec((1, tm, d), lambda b, i: (b, i, 0)),
        out_shape=jax.ShapeDtypeStruct((g, m, d), F32),
        compiler_params=_cparams("arbitrary", "arbitrary"),
        name="out_proj",
    )(mixed, w_out, x3, mod3, g_post.reshape(1, d))


def _pack_in_weights(w_in, w_gate):
    o = 0
    cols = {}
    for name, w in (("att_q", BRANCH_W), ("att_k", ATT_KV_W), ("att_v", ATT_KV_W), ("att_z", BRANCH_W),
                    ("ml_q", BRANCH_W), ("ml_k", BRANCH_W), ("ml_v", BRANCH_W), ("ml_o", BRANCH_W),
                    ("ml_z", BRANCH_W), ("ml_gates", 4 * ML_HEADS), ("ssm_xbc", SSM_CONV_CH),
                    ("ssm_dt", 2 * SSM_HEADS), ("ssm_z", BRANCH_W)):
        cols[name] = w_in[:, o:o + w]
        o += w
    perm = np.concatenate([np.arange(0, ATT_HD, 2), np.arange(1, ATT_HD, 2)])
    perm_q = (np.arange(ATT_HEADS)[:, None] * ATT_HD + perm[None, :]).reshape(-1)
    perm_k = (np.arange(ATT_KV_HEADS)[:, None] * ATT_HD + perm[None, :]).reshape(-1)
    big = jnp.concatenate([
        cols["ml_k"] * (ML_HD ** -0.5), cols["ml_v"], cols["ssm_xbc"],
        cols["att_k"][:, perm_k], cols["att_v"], cols["att_q"][:, perm_q], cols["att_z"],
        cols["ml_q"], cols["ml_o"], cols["ml_z"], cols["ssm_z"], w_gate], axis=1).astype(BF16)
    mg = cols["ml_gates"].reshape(-1, 4, ML_HEADS)
    zpad = lambda w: jnp.zeros((w_in.shape[0], w), F32)
    small = jnp.concatenate([
        mg[:, 0], mg[:, 2], zpad(LANES - 2 * ML_HEADS),
        mg[:, 1], mg[:, 3], zpad(LANES - 2 * ML_HEADS),
        cols["ssm_dt"], zpad(LANES - 2 * SSM_HEADS)], axis=1).astype(BF16)
    return big, small


def _rope_tables(seq):
    rows = seq // GRID_W
    row = jnp.repeat(jnp.arange(rows), GRID_W).astype(F32)
    col = jnp.tile(jnp.arange(GRID_W), rows).astype(F32)
    n_freq = ATT_HD // 4
    inv = ROPE_BASE ** (-jnp.arange(n_freq, dtype=F32) / n_freq)
    ang = jnp.concatenate([row[:, None] * inv, col[:, None] * inv], -1)
    cos2 = jnp.concatenate([jnp.cos(ang), jnp.cos(ang)], -1)
    sin2 = jnp.concatenate([-jnp.sin(ang), jnp.sin(ang)], -1)
    q_scale = ATT_SCALE * LOG2E
    return cos2 * q_scale, sin2 * q_scale, cos2, sin2


def _lane_row(v):
    v = v.astype(F32).reshape(1, -1)
    return jnp.pad(v, ((0, 0), (0, LANES - v.shape[1])))


def _ssd_params(dt_bias, a_log, d_skip, norm_g):
    chan_head = np.arange(BRANCH_W) // SSM_HD
    e = np.zeros((2, LANES, BRANCH_W), np.float32)
    for d in range(2):
        e[d, SSM_HEADS * d + chan_head, np.arange(BRANCH_W)] = 1.0
    even = ((chan_head % 2) == 0).astype(np.float32).reshape(1, BRANCH_W)
    dsk = jnp.repeat(d_skip.astype(F32), SSM_HD).reshape(1, BRANCH_W)
    return (_lane_row(dt_bias), _lane_row(-jnp.exp(a_log.astype(F32))), jnp.asarray(e, BF16),
            jnp.asarray(even), dsk, norm_g.astype(F32).reshape(1, BRANCH_W))


def kernel(x, c, ctx, c_ctx, w_mod, b_mod, g_pre, w_in, w_gate, b_gate, att_sink, ml_gate_bias,
           ml_norm, ssm_conv_w, ssm_conv_b, ssm_dt_bias, ssm_a_log, ssm_d, ssm_norm, w_branch,
           w_out, g_post):
    b, s, d = x.shape
    lc = ctx.shape[1]
    depth = w_mod.shape[0]
    rows = -(-(b + 1) // 8) * 8
    cc = jnp.concatenate([c, c_ctx[None, :], jnp.zeros((rows - b - 1, d), F32)], axis=0)
    mod = _modulation(cc, w_mod, b_mod)
    tables = _rope_tables(s)
    xc = ctx.reshape(1, b * lc, d)
    for l in range(depth):
        need_ctx = l < depth - 1
        w_big, w_small = _pack_in_weights(w_in[l], w_gate[l])
        mod_x = mod[l, :b].reshape(b, 3, d)
        mod_c = mod[l, b:b + 1].reshape(1, 3, d)
        u, us = _in_proj(x, mod_x, g_pre[l], w_big, w_small)
        uc, usc = _in_proj(xc, mod_c, g_pre[l], w_big, w_small,
                           U_COLS if need_ctx else U_STATE_COLS)
        uc = uc.reshape(b, lc, -1)
        usc = usc.reshape(b, lc, -1)

        sink = att_sink[l].astype(F32)
        o_a = _attention(u, uc, sink, tables)

        gb = ml_gate_bias[l]
        gbias = jnp.concatenate([_lane_row(jnp.concatenate([gb[0], gb[2]])),
                                 _lane_row(jnp.concatenate([gb[1], gb[3]]))], axis=1)
        st0 = (jnp.zeros((b, 2 * ML_HEADS, ML_HD, ML_CW), F32), jnp.zeros((b, 8, LANES), F32))
        sp = _ssd_params(ssm_dt_bias[l], ssm_a_log[l], ssm_d[l], ssm_norm[l])
        h0 = jnp.zeros((b, 2 * SSM_GROUPS, SSM_STATE, SSM_HPG * SSM_HD), F32)
        xbc_c = _ssm_conv(uc, ssm_conv_w[l], ssm_conv_b[l])
        if need_ctx:
            o_m_c, st = _mlstm(uc, usc, gbias, ml_norm[l].astype(F32), st0)
            o_s_c, hst = _ssd(xbc_c, uc, usc, sp, h0)
        else:
            st = _mlstm_state(uc, usc, gbias, st0)
            hst = _ssd_state(xbc_c, usc, sp, h0)
        o_m, _ = _mlstm(u, us, gbias, ml_norm[l].astype(F32), st)
        o_s, _ = _ssd(_ssm_conv(u, ssm_conv_w[l], ssm_conv_b[l]), u, us, sp, hst)

        wb = w_branch[l].astype(BF16)
        bg = b_gate[l].astype(F32).reshape(3, d)
        wo = w_out[l].astype(BF16)
        mixed = _branch_mix(o_a, o_m, o_s, u, wb, bg)
        x_new = _out_proj(mixed, wo, x, mod_x, g_post[l])
        if need_ctx:
            o_a_c = _ctx_attention(uc, sink)
            flat = lambda a: a.reshape(1, b * lc, a.shape[-1])
            mixed_c = _branch_mix(flat(o_a_c), flat(o_m_c), flat(o_s_c), flat(uc), wb, bg)
            xc = _out_proj(mixed_c, wo, xc, mod_c, g_post[l])
        x = x_new
    return x
```

```python
import functools

import numpy as np
import jax
import jax.numpy as jnp
from jax import lax
from jax.experimental import pallas as pl
from jax.experimental.pallas import tpu as pltpu

F32 = jnp.float32
BF16 = jnp.bfloat16
LOG2E = 1.4426950408889634

D_MODEL = 2048
EPS = 1e-6
BRANCH_W = D_MODEL // 2
GRID_W = 64
ROPE_BASE = 10000.0

ATT_HD = 128
ATT_HEADS = BRANCH_W // ATT_HD
ATT_KV_HEADS = ATT_HEADS // 4
ATT_REP = ATT_HEADS // ATT_KV_HEADS
ATT_KV_W = ATT_KV_HEADS * ATT_HD
ATT_BLOCK = 128
ATT_SCALE = ATT_HD ** -0.5

ML_HEADS = 4
ML_HD = BRANCH_W // ML_HEADS

SSM_HD = 64
SSM_HEADS = BRANCH_W // SSM_HD
SSM_GROUPS = 2
SSM_HPG = SSM_HEADS // SSM_GROUPS
SSM_STATE = 128
SSM_BC_W = SSM_GROUPS * SSM_STATE
SSM_CONV = 5
SSM_CONV_CH = BRANCH_W + 2 * SSM_BC_W

LANES = 128
CHUNK = LANES
NEG = -1e30
VMEM_LIMIT = 56 * 1024 * 1024

U_ML_K = 0
U_ML_V = 1024
U_SSM_X = 2048
U_SSM_B = 3072
U_SSM_C = 3328
U_ATT_K = 3584
U_ATT_V = 3840
U_ATT_Q = 4096
U_ATT_Z = 5120
U_ML_Q = 6144
U_ML_O = 7168
U_ML_Z = 8192
U_SSM_Z = 9216
U_GATE = 10240
U_COLS = U_GATE + 3 * D_MODEL
S_ML_I = 0
S_ML_F = LANES
S_SSM_DT = 2 * LANES
S_COLS = 3 * LANES
ML_CW = ML_HD + LANES

IN_TN = 1024
U_SILU_TILES = (U_ATT_Z // IN_TN, U_ML_Z // IN_TN, U_SSM_Z // IN_TN)
U_SIGMOID_TILE = U_ML_O // IN_TN
U_STATE_COLS = U_ATT_Q


def _cparams(*sem):
    return pltpu.CompilerParams(dimension_semantics=sem, vmem_limit_bytes=VMEM_LIMIT)


def _sigmoid(v):
    return 0.5 * jnp.tanh(0.5 * v) + 0.5


def _silu(v):
    return v * _sigmoid(v)


def _softplus(v):
    return jnp.maximum(v, 0.0) + jnp.log(1.0 + jnp.exp(-jnp.abs(v)))


def _log_sigmoid(v):
    return jnp.minimum(v, 0.0) - jnp.log(1.0 + jnp.exp(-jnp.abs(v)))


def _dot(a, b):
    return jnp.dot(a, b, preferred_element_type=F32)


def _dot_nt(a, b):
    return lax.dot_general(a, b, (((1,), (1,)), ((), ())), preferred_element_type=F32)


def _dot_tn(a, b):
    return lax.dot_general(a, b, (((0,), (0,)), ((), ())), preferred_element_type=F32)


def _scan_cumsum(tri_bf, v):
    hi = v.astype(BF16)
    r1 = v - hi.astype(F32)
    mid = r1.astype(BF16)
    lo = (r1 - mid.astype(F32)).astype(BF16)
    res = _dot(tri_bf, jnp.concatenate([hi, mid, lo], axis=1))
    return res[:, :LANES] + res[:, LANES:2 * LANES] + res[:, 2 * LANES:]


def _expand2(v, e_mat):
    hi = v.astype(BF16)
    lo = (v - hi.astype(F32)).astype(BF16)
    return _dot(hi, e_mat) + _dot(lo, e_mat)


def _mod_kernel(c_ref, w_ref, b_ref, o_ref):
    sc = _silu(c_ref[...]).astype(BF16)
    o_ref[0] = _dot(sc, w_ref[0].astype(BF16)) + b_ref[0]


def _modulation(cc, w_mod, b_mod):
    depth, d, n = w_mod.shape
    rows = cc.shape[0]
    tn = 1024
    return pl.pallas_call(
        _mod_kernel,
        grid=(depth, n // tn),
        in_specs=[pl.BlockSpec((rows, d), lambda l, j: (0, 0)),
                  pl.BlockSpec((1, d, tn), lambda l, j: (l, 0, j)),
                  pl.BlockSpec((1, 1, tn), lambda l, j: (l, 0, j))],
        out_specs=pl.BlockSpec((1, rows, tn), lambda l, j: (l, 0, j)),
        out_shape=jax.ShapeDtypeStruct((depth, rows, n), F32),
        compiler_params=_cparams("arbitrary", "arbitrary"),
        name="modulation",
    )(cc, w_mod, b_mod.reshape(depth, 1, n))


def _in_proj_kernel(x_ref, mod_ref, g_ref, w_ref, ws_ref, u_ref, us_ref, h_scr):
    j = pl.program_id(2)

    @pl.when(j == 0)
    def _():
        x = x_ref[0]
        ms = jnp.mean(x * x, axis=-1, keepdims=True)
        xn = x * lax.rsqrt(ms + EPS) * g_ref[...]
        h = xn * (1.0 + mod_ref[0, 1:2, :]) + mod_ref[0, 0:1, :]
        hb = h.astype(BF16)
        h_scr[...] = hb
        us_ref[0] = _dot(hb, ws_ref[...])

    is_silu = functools.reduce(jnp.logical_or, [j == t for t in U_SILU_TILES])
    is_sigmoid = j == U_SIGMOID_TILE

    @pl.when(is_silu)
    def _():
        u_ref[0] = _silu(_dot(h_scr[...], w_ref[...])).astype(BF16)

    @pl.when(is_sigmoid)
    def _():
        u_ref[0] = _sigmoid(_dot(h_scr[...], w_ref[...])).astype(BF16)

    @pl.when(jnp.logical_not(jnp.logical_or(is_silu, is_sigmoid)))
    def _():
        u_ref[0] = _dot(h_scr[...], w_ref[...]).astype(BF16)


def _in_proj(x3, mod3, g_pre, w_big, w_small, n=U_COLS):
    g, m, d = x3.shape
    tm = min(1024, m)
    return pl.pallas_call(
        _in_proj_kernel,
        grid=(g, m // tm, n // IN_TN),
        in_specs=[pl.BlockSpec((1, tm, d), lambda b, i, j: (b, i, 0)),
                  pl.BlockSpec((1, 3, d), lambda b, i, j: (b, 0, 0)),
                  pl.BlockSpec((1, d), lambda b, i, j: (0, 0)),
                  pl.BlockSpec((d, IN_TN), lambda b, i, j: (0, j)),
                  pl.BlockSpec((d, S_COLS), lambda b, i, j: (0, 0))],
        out_specs=[pl.BlockSpec((1, tm, IN_TN), lambda b, i, j: (b, i, j)),
                   pl.BlockSpec((1, tm, S_COLS), lambda b, i, j: (b, i, 0))],
        out_shape=[jax.ShapeDtypeStruct((g, m, n), BF16),
                   jax.ShapeDtypeStruct((g, m, S_COLS), F32)],
        scratch_shapes=[pltpu.VMEM((tm, d), BF16)],
        compiler_params=_cparams("arbitrary", "arbitrary", "arbitrary"),
        name="in_proj",
    )(x3, mod3, g_pre.reshape(1, d), w_big, w_small)


def _att_kernel(sink_ref, q_ref, z_ref, kp_ref, kc_ref, kn_ref, vp_ref, vc_ref, vn_ref,
                ctxk_ref, ctxv_ref, cq_ref, sq_ref, ck_ref, sk_ref, o_ref, *, nb):
    i = pl.program_id(1)
    blk = ATT_BLOCK
    r0 = pl.multiple_of(i * blk, blk)
    rp = pl.multiple_of(jnp.maximum(i - 1, 0) * blk, blk)
    rn = pl.multiple_of(jnp.minimum(i + 1, nb - 1) * blk, blk)

    def rope(t, cos2, sin2):
        return t * cos2 + pltpu.roll(t, ATT_HD // 2, 1) * sin2

    cq = cq_ref[pl.ds(r0, blk), :]
    sq = sq_ref[pl.ds(r0, blk), :]
    key = lax.broadcasted_iota(jnp.int32, (blk, blk), 0)
    qry = lax.broadcasted_iota(jnp.int32, (blk, blk), 1)
    bias_p = jnp.where(key >= qry, 0.0, NEG) + jnp.where(i > 0, 0.0, NEG)
    bias_n = jnp.where(key <= qry, 0.0, NEG) + jnp.where(i < nb - 1, 0.0, NEG)
    bias_p = jnp.concatenate([bias_p, bias_p], axis=1)
    bias_n = jnp.concatenate([bias_n, bias_n], axis=1)
    scores = []
    for g in range(ATT_KV_HEADS):
        ks = slice(g * ATT_HD, (g + 1) * ATT_HD)
        kparts = []
        for k_ref, r in ((kp_ref, rp), (kc_ref, r0), (kn_ref, rn)):
            kr = rope(k_ref[0][:, ks].astype(F32), ck_ref[pl.ds(r, blk), :], sk_ref[pl.ds(r, blk), :])
            kparts.append(kr.astype(BF16))
        kk = jnp.concatenate(kparts + [ctxk_ref[0][:, ks]], axis=0)
        vv = jnp.concatenate([vp_ref[0][:, ks], vc_ref[0][:, ks], vn_ref[0][:, ks],
                              ctxv_ref[0][:, ks]], axis=0)
        vv_t = vv.T
        for pair in range(ATT_REP // 2):
            heads = [g * ATT_REP + 2 * pair + j for j in range(2)]
            q2 = jnp.concatenate(
                [rope(q_ref[0][:, hh * ATT_HD:(hh + 1) * ATT_HD].astype(F32), cq, sq).astype(BF16)
                 for hh in heads], axis=0)
            scores.append((heads, vv_t, _dot_nt(kk, q2)))
    for heads, vv_t, s_t in scores:
        sink = jnp.concatenate([jnp.full((1, blk), sink_ref[hh] * LOG2E, F32) for hh in heads],
                               axis=1)
        s_t = jnp.concatenate([s_t[:blk] + bias_p, s_t[blk:2 * blk],
                               s_t[2 * blk:3 * blk] + bias_n, s_t[3 * blk:]], axis=0)
        m = jnp.maximum(jnp.max(s_t, axis=0, keepdims=True), sink)
        e_t = jnp.exp2(s_t - m)
        den = jnp.sum(e_t, axis=0, keepdims=True) + jnp.exp2(sink - m)
        o_t = _dot(vv_t, e_t.astype(BF16)) / den
        for j, hh in enumerate(heads):
            hs = slice(hh * ATT_HD, (hh + 1) * ATT_HD)
            o = o_t[:, j * blk:(j + 1) * blk].T
            o_ref[0, :, hs] = (o * z_ref[0][:, hs].astype(F32)).astype(BF16)


def _attention(u, uc, sink, tables):
    b, s, _ = u.shape
    lc = uc.shape[1]
    nb = s // ATT_BLOCK
    blk = ATT_BLOCK
    kcol = U_ATT_K // ATT_KV_W
    vcol = U_ATT_V // ATT_KV_W

    def kv_spec(col, shift):
        return pl.BlockSpec((1, blk, ATT_KV_W),
                            lambda bb, i: (bb, jnp.clip(i + shift, 0, nb - 1), col))

    tab_spec = pl.BlockSpec((s, ATT_HD), lambda bb, i: (0, 0))
    return pl.pallas_call(
        functools.partial(_att_kernel, nb=nb),
        grid=(b, nb),
        in_specs=[pl.BlockSpec(memory_space=pltpu.SMEM),
                  pl.BlockSpec((1, blk, BRANCH_W), lambda bb, i: (bb, i, U_ATT_Q // BRANCH_W)),
                  pl.BlockSpec((1, blk, BRANCH_W), lambda bb, i: (bb, i, U_ATT_Z // BRANCH_W)),
                  kv_spec(kcol, -1), kv_spec(kcol, 0), kv_spec(kcol, 1),
                  kv_spec(vcol, -1), kv_spec(vcol, 0), kv_spec(vcol, 1),
                  pl.BlockSpec((1, lc, ATT_KV_W), lambda bb, i: (bb, 0, kcol)),
                  pl.BlockSpec((1, lc, ATT_KV_W), lambda bb, i: (bb, 0, vcol)),
                  tab_spec, tab_spec, tab_spec, tab_spec],
        out_specs=pl.BlockSpec((1, blk, BRANCH_W), lambda bb, i: (bb, i, 0)),
        out_shape=jax.ShapeDtypeStruct((b, s, BRANCH_W), BF16),
        compiler_params=_cparams("arbitrary", "arbitrary"),
        name="window_attention",
    )(sink, u, u, u, u, u, u, u, u, uc, uc, *tables)


def _ctx_att_kernel(sink_ref, q_ref, z_ref, k_ref, v_ref, o_ref):
    for hh in range(ATT_HEADS):
        g = hh // ATT_REP
        ks = slice(g * ATT_HD, (g + 1) * ATT_HD)
        hs = slice(hh * ATT_HD, (hh + 1) * ATT_HD)
        qh = (q_ref[0][:, hs].astype(F32) * ATT_SCALE).astype(BF16)
        sink = sink_ref[hh]
        s = _dot_nt(qh, k_ref[0][:, ks])
        m = jnp.maximum(jnp.max(s, axis=1, keepdims=True), sink)
        e = jnp.exp(s - m)
        den = jnp.sum(e, axis=1, keepdims=True) + jnp.exp(sink - m)
        o = _dot(e.astype(BF16), v_ref[0][:, ks]) / den
        o_ref[0, :, hs] = (o * z_ref[0][:, hs].astype(F32)).astype(BF16)


def _ctx_attention(uc, sink):
    b, lc, _ = uc.shape
    kcol = U_ATT_K // ATT_KV_W
    vcol = U_ATT_V // ATT_KV_W
    return pl.pallas_call(
        _ctx_att_kernel,
        grid=(b,),
        in_specs=[pl.BlockSpec(memory_space=pltpu.SMEM),
                  pl.BlockSpec((1, lc, BRANCH_W), lambda bb: (bb, 0, U_ATT_Q // BRANCH_W)),
                  pl.BlockSpec((1, lc, BRANCH_W), lambda bb: (bb, 0, U_ATT_Z // BRANCH_W)),
                  pl.BlockSpec((1, lc, ATT_KV_W), lambda bb: (bb, 0, kcol)),
                  pl.BlockSpec((1, lc, ATT_KV_W), lambda bb: (bb, 0, vcol))],
        out_specs=pl.BlockSpec((1, lc, BRANCH_W), lambda bb: (bb, 0, 0)),
        out_shape=jax.ShapeDtypeStruct((b, lc, BRANCH_W), BF16),
        compiler_params=_cparams("arbitrary"),
        name="context_attention",
    )(sink, uc, uc, uc, uc)


def _scan_masks(reverse, n):
    row = lax.broadcasted_iota(jnp.int32, (n, n), 0)
    col = lax.broadcasted_iota(jnp.int32, (n, n), 1)
    tri = (col >= row) if reverse else (col <= row)
    return tri, jnp.where(tri, 1.0, 0.0).astype(BF16)


def _cummax_rows(a, reverse):
    n = a.shape[0]
    row = lax.broadcasted_iota(jnp.int32, a.shape, 0)
    k = 1
    while k < n:
        if reverse:
            a = jnp.maximum(a, jnp.where(row < n - k, pltpu.roll(a, n - k, 0), NEG))
        else:
            a = jnp.maximum(a, jnp.where(row >= k, pltpu.roll(a, k, 0), NEG))
        k *= 2
    return a


def _run_skewed(gens):
    pending = list(gens)
    active = []
    while pending or active:
        if pending:
            active.append(pending.pop(0))
        for gen in list(active):
            if next(gen, StopIteration) is StopIteration:
                active.remove(gen)


def _mlstm_gates(reverse, g_ref, gbias_ref, m_row):
    n = CHUNK
    last = 0 if reverse else n - 1
    tri, trif = _scan_masks(reverse, n)
    g_all = g_ref[0] + gbias_ref[...]
    b_all = _scan_cumsum(trif, _log_sigmoid(g_all[:, S_ML_F:S_ML_F + LANES]))
    a_all = g_all[:, S_ML_I:S_ML_I + LANES] - b_all
    mx_all = jnp.maximum(_cummax_rows(a_all, reverse), m_row)
    mt_all = b_all + mx_all
    return dict(tri=tri, mx=mx_all, fl=jnp.exp(-mt_all), mx_last=mx_all[last:last + 1, :],
                a_t=a_all.T, m_new=mt_all[last:last + 1, :])


def _mlstm_head(gates, ln, hs, q_ref, k_ref, v_ref, c_ref, m_row, outs):
    n = CHUNK
    a_row = gates["a_t"][ln:ln + 1, :]
    m_prev = m_row[:, ln:ln + 1]
    mxl = gates["mx_last"][:, ln:ln + 1]
    k_t = k_ref[0][:, hs].T
    v1 = jnp.concatenate([v_ref[0][:, hs], jnp.ones((n, LANES), BF16)], axis=1)
    c_prev = c_ref[0, ln]
    if q_ref is not None:
        qh = q_ref[0][:, hs]
        s = _dot(qh, k_t)
        qc = _dot(qh, c_prev.astype(BF16))
    yield
    kw_t = (k_t.astype(F32) * jnp.exp(a_row - mxl)).astype(BF16)
    upd = _dot(kw_t, v1)
    yield
    if q_ref is not None:
        mx_b = jnp.broadcast_to(gates["mx"][:, ln:ln + 1], (n, LANES))
        w = jnp.exp(jnp.where(gates["tri"], a_row - mx_b, NEG))
        pv = _dot((s * w).astype(BF16), v1)
        yield
    c_ref[0, ln] = jnp.exp(m_prev - mxl) * c_prev + upd
    if q_ref is not None:
        yield
        fl_b = jnp.broadcast_to(gates["fl"][:, ln:ln + 1], (n, LANES))
        e_int = jnp.exp(m_prev - mx_b)
        nd = jnp.concatenate([e_int] * (ML_CW // LANES), axis=1) * qc + pv
        r = 1.0 / jnp.maximum(jnp.abs(nd[:, ML_HD:]), fl_b)
        outs[ln] = nd[:, :ML_HD] * jnp.concatenate([r] * (ML_HD // LANES), axis=1)


def _mlstm_step(fwd_refs, bwd_refs, gbias_ref, c_ref, m_row):
    gates = [_mlstm_gates(bool(d), refs[3], gbias_ref, m_row)
             for d, refs in enumerate((fwd_refs, bwd_refs))]
    outs = {}
    gens = []
    for h in range(ML_HEADS):
        for d, refs in enumerate((fwd_refs, bwd_refs)):
            gens.append(_mlstm_head(gates[d], d * ML_HEADS + h, slice(h * ML_HD, (h + 1) * ML_HD),
                                    refs[0], refs[1], refs[2], c_ref, m_row, outs))
    _run_skewed(gens)
    lane = lax.broadcasted_iota(jnp.int32, (1, LANES), 1)
    m_new = jnp.where(lane < ML_HEADS, gates[0]["m_new"], gates[1]["m_new"])
    if fwd_refs[0] is None:
        return None, None, m_new
    h_f = jnp.concatenate([outs[h] for h in range(ML_HEADS)], axis=1)
    h_b = jnp.concatenate([outs[ML_HEADS + h] for h in range(ML_HEADS)], axis=1)
    return h_f, h_b, m_new


def _mlstm_finish(h_sum, o_ref, z_ref, ng_ref):
    t = o_ref[0].astype(F32) * h_sum
    parts = []
    for h in range(ML_HEADS):
        th = t[:, h * ML_HD:(h + 1) * ML_HD]
        parts.append(th * lax.rsqrt(jnp.mean(th * th, axis=1, keepdims=True) + EPS))
    hn = jnp.concatenate(parts, axis=1)
    return (hn * ng_ref[...] * z_ref[0].astype(F32)).astype(BF16)


def _mlstm_kernel(qf, kf, vf, gf, of, zf, qb, kb, vb, gb, ob, zb, gbias, ng, c0, m0,
                  out, cfin, mfin, h_scr, *, nc):
    s = pl.program_id(1)

    @pl.when(s == 0)
    def _():
        cfin[...] = c0[...]
        mfin[...] = m0[...]

    m_row = mfin[0, 0:1, :]
    h_f, h_b, m_new = _mlstm_step((qf, kf, vf, gf), (qb, kb, vb, gb), gbias, cfin, m_row)
    mfin[0] = jnp.broadcast_to(m_new, (8, LANES))
    rf = pl.multiple_of(s * CHUNK, CHUNK)
    rb = pl.multiple_of((nc - 1 - s) * CHUNK, CHUNK)

    @pl.when(s < nc // 2)
    def _():
        h_scr[pl.ds(rf, CHUNK), :] = h_f
        h_scr[pl.ds(rb, CHUNK), :] = h_b

    @pl.when(s >= nc // 2)
    def _():
        out[0, pl.ds(rf, CHUNK), :] = _mlstm_finish(h_f + h_scr[pl.ds(rf, CHUNK), :], of, zf, ng)
        out[0, pl.ds(rb, CHUNK), :] = _mlstm_finish(h_b + h_scr[pl.ds(rb, CHUNK), :], ob, zb, ng)


def _mlstm_state_kernel(kf, vf, gf, kb, vb, gb, gbias, c0, m0, cfin, mfin):
    @pl.when(pl.program_id(1) == 0)
    def _():
        cfin[...] = c0[...]
        mfin[...] = m0[...]

    m_row = mfin[0, 0:1, :]
    _, _, m_new = _mlstm_step((None, kf, vf, gf), (None, kb, vb, gb), gbias, cfin, m_row)
    mfin[0] = jnp.broadcast_to(m_new, (8, LANES))


def _mlstm_state(u, us, gbias_row, state):
    b, t, _ = u.shape
    nc = t // CHUNK
    c0, m0 = state
    gw = 2 * LANES

    def chunk(width, col, rev):
        if rev:
            return pl.BlockSpec((1, CHUNK, width), lambda bb, s: (bb, nc - 1 - s, col))
        return pl.BlockSpec((1, CHUNK, width), lambda bb, s: (bb, s, col))

    def dir_specs(rev):
        return [chunk(BRANCH_W, U_ML_K // BRANCH_W, rev), chunk(BRANCH_W, U_ML_V // BRANCH_W, rev),
                chunk(gw, 0, rev)]

    c_spec = pl.BlockSpec((1, 2 * ML_HEADS, ML_HD, ML_CW), lambda bb, s: (bb, 0, 0, 0))
    m_spec = pl.BlockSpec((1, 8, LANES), lambda bb, s: (bb, 0, 0))
    cf, mf = pl.pallas_call(
        _mlstm_state_kernel,
        grid=(b, nc),
        in_specs=dir_specs(False) + dir_specs(True) + [
            pl.BlockSpec((1, gw), lambda bb, s: (0, 0)), c_spec, m_spec],
        out_specs=[c_spec, m_spec],
        out_shape=[jax.ShapeDtypeStruct(c0.shape, F32), jax.ShapeDtypeStruct(m0.shape, F32)],
        compiler_params=_cparams("arbitrary", "arbitrary"),
        name="mlstm_state",
    )(u, u, us, u, u, us, gbias_row, c0, m0)
    return cf, mf


def _mlstm(u, us, gbias_row, norm_g, state):
    b, t, _ = u.shape
    nc = t // CHUNK
    assert nc % 2 == 0
    c0, m0 = state
    nst = 2 * ML_HEADS
    gw = 2 * LANES

    def fwd(col):
        return pl.BlockSpec((1, CHUNK, BRANCH_W), lambda bb, s: (bb, s, col // BRANCH_W))

    def bwd(col):
        return pl.BlockSpec((1, CHUNK, BRANCH_W), lambda bb, s: (bb, nc - 1 - s, col // BRANCH_W))

    gate_f = pl.BlockSpec((1, CHUNK, gw), lambda bb, s: (bb, s, 0))
    gate_b = pl.BlockSpec((1, CHUNK, gw), lambda bb, s: (bb, nc - 1 - s, 0))
    c_spec = pl.BlockSpec((1, nst, ML_HD, ML_CW), lambda bb, s: (bb, 0, 0, 0))
    m_spec = pl.BlockSpec((1, 8, LANES), lambda bb, s: (bb, 0, 0))
    out, cf, mf = pl.pallas_call(
        functools.partial(_mlstm_kernel, nc=nc),
        grid=(b, nc),
        in_specs=[fwd(U_ML_Q), fwd(U_ML_K), fwd(U_ML_V), gate_f, fwd(U_ML_O), fwd(U_ML_Z),
                  bwd(U_ML_Q), bwd(U_ML_K), bwd(U_ML_V), gate_b, bwd(U_ML_O), bwd(U_ML_Z),
                  pl.BlockSpec((1, gw), lambda bb, s: (0, 0)),
                  pl.BlockSpec((1, BRANCH_W), lambda bb, s: (0, 0)),
                  c_spec, m_spec],
        out_specs=[pl.BlockSpec((1, t, BRANCH_W), lambda bb, s: (bb, 0, 0)), c_spec, m_spec],
        out_shape=[jax.ShapeDtypeStruct((b, t, BRANCH_W), BF16),
                   jax.ShapeDtypeStruct(c0.shape, F32),
                   jax.ShapeDtypeStruct(m0.shape, F32)],
        scratch_shapes=[pltpu.VMEM((t, BRANCH_W), F32)],
        compiler_params=_cparams("arbitrary", "arbitrary"),
        name="mlstm",
    )(u, u, u, us, u, u, u, u, u, us, u, u, gbias_row, norm_g.reshape(1, BRANCH_W), c0, m0)
    return out, (cf, mf)


CONV_TN = 512
CONV_PAD = 16


def _conv_kernel(x_ref, w_ref, b_ref, o_ref, pad_scr, *, t):
    pad_scr[0:CONV_PAD, :] = jnp.zeros((CONV_PAD, CONV_TN), F32)
    pad_scr[CONV_PAD + t:2 * CONV_PAD + t, :] = jnp.zeros((CONV_PAD, CONV_TN), F32)
    pad_scr[CONV_PAD:CONV_PAD + t, :] = x_ref[0].astype(F32)
    rows = 256
    half = SSM_CONV // 2
    halo = 8
    for c in range(t // rows):
        r0 = c * rows
        win = pad_scr[CONV_PAD + r0 - halo:CONV_PAD + r0 + rows + halo, :]
        acc = jnp.broadcast_to(b_ref[...], (rows, CONV_TN))
        for k in range(SSM_CONV):
            sh = win if k == half else pltpu.roll(win, (half - k) % (rows + 2 * halo), 0)
            acc = acc + w_ref[k:k + 1, :] * sh[halo:halo + rows, :]
        o_ref[0, r0:r0 + rows, :] = _silu(acc).astype(BF16)


def _ssm_conv(u, conv_w, conv_b):
    b, t, _ = u.shape
    w8 = jnp.zeros((8, SSM_CONV_CH), F32).at[:SSM_CONV].set(conv_w)
    first = U_SSM_X // CONV_TN
    return pl.pallas_call(
        functools.partial(_conv_kernel, t=t),
        grid=(b, SSM_CONV_CH // CONV_TN),
        in_specs=[pl.BlockSpec((1, t, CONV_TN), lambda bb, j: (bb, 0, first + j)),
                  pl.BlockSpec((8, CONV_TN), lambda bb, j: (0, j)),
                  pl.BlockSpec((1, CONV_TN), lambda bb, j: (0, j))],
        out_specs=pl.BlockSpec((1, t, CONV_TN), lambda bb, j: (bb, 0, j)),
        out_shape=jax.ShapeDtypeStruct((b, t, SSM_CONV_CH), BF16),
        scratch_shapes=[pltpu.VMEM((t + 2 * CONV_PAD, CONV_TN), F32)],
        compiler_params=_cparams("arbitrary", "arbitrary"),
        name="ssm_conv",
    )(u, w8, conv_b.reshape(1, SSM_CONV_CH))


def _ssd_dir(reverse, x_ref, bm_ref, cm_ref, dt_ref, dtb_ref, a_ref, e_ref, even_ref, h_ref):
    d = 1 if reverse else 0
    n = CHUNK
    last = 0 if reverse else n - 1
    tri, trif = _scan_masks(reverse, n)
    dt_all = _softplus(dt_ref[0] + dtb_ref[...])
    cum = _scan_cumsum(trif, dt_all * a_ref[...])
    total = cum[last:last + 1, :]
    cum2 = cum * LOG2E
    src2_t = (cum2 - jnp.log2(dt_all)).T
    e_mat = e_ref[d]
    wd_x = _dot((jnp.exp(total - cum) * dt_all).astype(BF16), e_mat)
    tot_x = _expand2(jnp.broadcast_to(jnp.exp(total), (8, LANES)), e_mat)[0:1, :]

    x = x_ref[0]
    pro = dict(tri=tri, cum2=cum2, src2_t=src2_t, x=x, d=d, e_mat=e_mat, tot_x=tot_x,
               xw=(x.astype(F32) * wd_x).astype(BF16), bm_ref=bm_ref, cm_ref=cm_ref, h_ref=h_ref)
    if cm_ref is not None:
        even = even_ref[...].astype(BF16)
        pro["x_par"] = (x * even, x * (1.0 - even))
        pro["dec"] = jnp.exp(cum)
        pro["cb"] = [_dot_nt(cm_ref[0][:, g * SSM_STATE:(g + 1) * SSM_STATE],
                             bm_ref[0][:, g * SSM_STATE:(g + 1) * SSM_STATE])
                     for g in range(SSM_GROUPS)]
    return pro


def _ssd_head(pro, g, p, par, terms):
    n = CHUNK
    ln = SSM_HEADS * pro["d"] + g * SSM_HPG + 2 * p + par
    lo = (g * SSM_HPG + 2 * p) * SSM_HD
    c_b = jnp.broadcast_to(pro["cum2"][:, ln:ln + 1], (n, LANES))
    seg = jnp.where(pro["tri"], c_b - pro["src2_t"][ln:ln + 1, :], NEG)
    yield
    mix = (pro["cb"][g] * jnp.exp2(seg)).astype(BF16)
    yield
    term = _dot(mix, pro["x_par"][par][:, lo:lo + 2 * SSM_HD])
    key = (pro["d"], g, p)
    terms[key] = term if par == 0 else terms[key] + term


def _ssd_group_end(pro, g, terms, ys):
    yield
    yield
    gw = SSM_HPG * SSM_HD
    gs = slice(g * gw, (g + 1) * gw)
    d = pro["d"]
    ss = slice(g * SSM_STATE, (g + 1) * SSM_STATE)
    h_prev = pro["h_ref"][0, d * SSM_GROUPS + g]
    if pro["cm_ref"] is not None:
        inter = _dot(pro["cm_ref"][0][:, ss], h_prev.astype(BF16))
        dec_x = _dot(pro["dec"].astype(BF16), pro["e_mat"][:, gs])
        ys[(d, g)] = (jnp.concatenate([terms[(d, g, p)] for p in range(SSM_HPG // 2)], axis=1)
                      + dec_x * inter)
    pro["h_ref"][0, d * SSM_GROUPS + g] = (pro["tot_x"][:, gs] * h_prev
                                           + _dot_tn(pro["bm_ref"][0][:, ss], pro["xw"][:, gs]))
    yield


def _ssd_step(fwd_refs, bwd_refs, dtb_ref, a_ref, e_ref, even_ref, h_ref):
    pros = [_ssd_dir(bool(d), refs[0], refs[1], refs[2], refs[3], dtb_ref, a_ref, e_ref, even_ref,
                     h_ref) for d, refs in enumerate((fwd_refs, bwd_refs))]
    with_out = fwd_refs[2] is not None
    terms, ys, gens = {}, {}, []
    for g in range(SSM_GROUPS):
        if with_out:
            gens += [_ssd_head(pros[d], g, p, par, terms)
                     for p in range(SSM_HPG // 2) for par in range(2) for d in range(2)]
        gens += [_ssd_group_end(pros[d], g, terms, ys) for d in range(2)]
    _run_skewed(gens)
    if not with_out:
        return None
    y_f, y_b = (jnp.concatenate([ys[(d, g)] for g in range(SSM_GROUPS)], axis=1) for d in range(2))
    return y_f, pros[0]["x"], y_b, pros[1]["x"]


def _ssd_finish(y_sum, x, z_ref, dsk_ref, ng_ref):
    y = (y_sum + dsk_ref[...] * x.astype(F32)) * z_ref[0].astype(F32)
    y = y * lax.rsqrt(jnp.mean(y * y, axis=1, keepdims=True) + EPS)
    return (y * ng_ref[...]).astype(BF16)


def _ssd_kernel(xf, bf, cf, sf, zf, xb, bb_, cb_, sb, zb, dtb, arow, emat, even, dsk, ng, h0,
                out, hfin, y_scr, *, nc):
    s = pl.program_id(1)

    @pl.when(s == 0)
    def _():
        hfin[...] = h0[...]

    y_f, x_f, y_b, x_b = _ssd_step((xf, bf, cf, sf), (xb, bb_, cb_, sb), dtb, arow, emat, even, hfin)
    rf = pl.multiple_of(s * CHUNK, CHUNK)
    rb = pl.multiple_of((nc - 1 - s) * CHUNK, CHUNK)

    @pl.when(s < nc // 2)
    def _():
        y_scr[pl.ds(rf, CHUNK), :] = y_f
        y_scr[pl.ds(rb, CHUNK), :] = y_b

    @pl.when(s >= nc // 2)
    def _():
        out[0, pl.ds(rf, CHUNK), :] = _ssd_finish(y_f + y_scr[pl.ds(rf, CHUNK), :], x_f, zf, dsk, ng)
        out[0, pl.ds(rb, CHUNK), :] = _ssd_finish(y_b + y_scr[pl.ds(rb, CHUNK), :], x_b, zb, dsk, ng)


def _ssd_state_kernel(xf, bf, sf, xb, bb_, sb, dtb, arow, emat, h0, hfin):
    @pl.when(pl.program_id(1) == 0)
    def _():
        hfin[...] = h0[...]

    _ssd_step((xf, bf, None, sf), (xb, bb_, None, sb), dtb, arow, emat, None, hfin)


def _ssd_state(xbc, us, params, h0):
    b, t, _ = xbc.shape
    nc = t // CHUNK
    dtb_row, a_row, e_mat = params[:3]

    def chunk(width, col, rev):
        if rev:
            return pl.BlockSpec((1, CHUNK, width), lambda bb, s: (bb, nc - 1 - s, col))
        return pl.BlockSpec((1, CHUNK, width), lambda bb, s: (bb, s, col))

    def dir_specs(rev):
        return [chunk(BRANCH_W, 0, rev), chunk(SSM_BC_W, BRANCH_W // SSM_BC_W, rev),
                chunk(LANES, S_SSM_DT // LANES, rev)]

    row_spec = pl.BlockSpec((1, LANES), lambda bb, s: (0, 0))
    h_spec = pl.BlockSpec((1, 2 * SSM_GROUPS, SSM_STATE, SSM_HPG * SSM_HD),
                          lambda bb, s: (bb, 0, 0, 0))
    return pl.pallas_call(
        _ssd_state_kernel,
        grid=(b, nc),
        in_specs=dir_specs(False) + dir_specs(True) + [
            row_spec, row_spec, pl.BlockSpec((2, LANES, BRANCH_W), lambda bb, s: (0, 0, 0)), h_spec],
        out_specs=h_spec,
        out_shape=jax.ShapeDtypeStruct(h0.shape, F32),
        compiler_params=_cparams("arbitrary", "arbitrary"),
        name="ssd_state",
    )(xbc, xbc, us, xbc, xbc, us, dtb_row, a_row, e_mat, h0)


def _ssd(xbc, u, us, params, h0):
    b, t, _ = u.shape
    nc = t // CHUNK
    assert nc % 2 == 0
    dtb_row, a_row, e_mat, even_row, dsk_row, ng_row = params
    nst = 2 * SSM_GROUPS
    gw = SSM_HPG * SSM_HD

    def chunk_spec(width, col, rev):
        if rev:
            return pl.BlockSpec((1, CHUNK, width), lambda bb, s: (bb, nc - 1 - s, col))
        return pl.BlockSpec((1, CHUNK, width), lambda bb, s: (bb, s, col))

    def dir_specs(rev):
        return [chunk_spec(BRANCH_W, 0, rev),
                chunk_spec(SSM_BC_W, BRANCH_W // SSM_BC_W, rev),
                chunk_spec(SSM_BC_W, BRANCH_W // SSM_BC_W + 1, rev),
                chunk_spec(LANES, S_SSM_DT // LANES, rev),
                chunk_spec(BRANCH_W, U_SSM_Z // BRANCH_W, rev)]

    def row_spec(width):
        return pl.BlockSpec((1, width), lambda bb, s: (0, 0))

    h_spec = pl.BlockSpec((1, nst, SSM_STATE, gw), lambda bb, s: (bb, 0, 0, 0))
    out, hf = pl.pallas_call(
        functools.partial(_ssd_kernel, nc=nc),
        grid=(b, nc),
        in_specs=dir_specs(False) + dir_specs(True) + [
            row_spec(LANES), row_spec(LANES),
            pl.BlockSpec((2, LANES, BRANCH_W), lambda bb, s: (0, 0, 0)),
            row_spec(BRANCH_W), row_spec(BRANCH_W), row_spec(BRANCH_W), h_spec],
        out_specs=[pl.BlockSpec((1, t, BRANCH_W), lambda bb, s: (bb, 0, 0)), h_spec],
        out_shape=[jax.ShapeDtypeStruct((b, t, BRANCH_W), BF16),
                   jax.ShapeDtypeStruct(h0.shape, F32)],
        scratch_shapes=[pltpu.VMEM((t, BRANCH_W), F32)],
        compiler_params=_cparams("arbitrary", "arbitrary"),
        name="ssd",
    )(xbc, xbc, xbc, us, u, xbc, xbc, xbc, us, u,
      dtb_row, a_row, e_mat, even_row, dsk_row, ng_row, h0)
    return out, hf


MIX_TN = 1024


def _branch_mix_kernel(oa_ref, om_ref, os_ref, wb_ref, ga_ref, gm_ref, gs_ref, bg_ref, out_ref):
    acc = None
    for n, (o_ref, g_ref) in enumerate(((oa_ref, ga_ref), (om_ref, gm_ref), (os_ref, gs_ref))):
        gate = _sigmoid(g_ref[0].astype(F32) + bg_ref[n:n + 1, :])
        term = gate * _dot(o_ref[0], wb_ref[n])
        acc = term if acc is None else acc + term
    out_ref[0] = acc.astype(BF16)


def _branch_mix(o_a, o_m, o_s, u, w_branch, b_gate3):
    g, m, _ = o_a.shape
    tm = min(1024, m)
    d = D_MODEL
    o_spec = pl.BlockSpec((1, tm, BRANCH_W), lambda b, i, j: (b, i, 0))

    def gate_spec(n):
        base = (U_GATE + n * d) // MIX_TN
        return pl.BlockSpec((1, tm, MIX_TN), lambda b, i, j: (b, i, base + j))

    return pl.pallas_call(
        _branch_mix_kernel,
        grid=(g, m // tm, d // MIX_TN),
        in_specs=[o_spec, o_spec, o_spec,
                  pl.BlockSpec((3, BRANCH_W, MIX_TN), lambda b, i, j: (0, 0, j)),
                  gate_spec(0), gate_spec(1), gate_spec(2),
                  pl.BlockSpec((3, MIX_TN), lambda b, i, j: (0, j))],
        out_specs=pl.BlockSpec((1, tm, MIX_TN), lambda b, i, j: (b, i, j)),
        out_shape=jax.ShapeDtypeStruct((g, m, d), BF16),
        compiler_params=_cparams("arbitrary", "arbitrary", "arbitrary"),
        name="branch_mix",
    )(o_a, o_m, o_s, w_branch, u, u, u, b_gate3)


def _out_proj_kernel(mix_ref, w_ref, x_ref, mod_ref, g_ref, out_ref):
    y = _dot(mix_ref[0], w_ref[...])
    yn = y * lax.rsqrt(jnp.mean(y * y, axis=-1, keepdims=True) + EPS) * g_ref[...]
    out_ref[0] = x_ref[0] + mod_ref[0, 2:3, :] * yn


def _out_proj(mixed, w_out, x3, mod3, g_post):
    g, m, d = x3.shape
    tm = min(512, m)
    return pl.pallas_call(
        _out_proj_kernel,
        grid=(g, m // tm),
        in_specs=[pl.BlockSpec((1, tm, d), lambda b, i: (b, i, 0)),
                  pl.BlockSpec((d, d), lambda b, i: (0, 0)),
                  pl.BlockSpec((1, tm, d), lambda b, i: (b, i, 0)),
                  pl.BlockSpec((1, 3, d), lambda b, i: (b, 0, 0)),
                  pl.BlockSpec((1, d), lambda b, i: (0, 0))],
        out_specs=pl.BlockSpec((1, tm, d), lambda b, i: (b, i, 0)),
        out_shape=jax.ShapeDtypeStruct((g, m, d), F32),
        compiler_params=_cparams("arbitrary", "arbitrary"),
        name="out_proj",
    )(mixed, w_out, x3, mod3, g_post.reshape(1, d))


def _pack_in_weights(w_in, w_gate):
    o = 0
    cols = {}
    for name, w in (("att_q", BRANCH_W), ("att_k", ATT_KV_W), ("att_v", ATT_KV_W), ("att_z", BRANCH_W),
                    ("ml_q", BRANCH_W), ("ml_k", BRANCH_W), ("ml_v", BRANCH_W), ("ml_o", BRANCH_W),
                    ("ml_z", BRANCH_W), ("ml_gates", 4 * ML_HEADS), ("ssm_xbc", SSM_CONV_CH),
                    ("ssm_dt", 2 * SSM_HEADS), ("ssm_z", BRANCH_W)):
        cols[name] = w_in[:, o:o + w]
        o += w
    perm = np.concatenate([np.arange(0, ATT_HD, 2), np.arange(1, ATT_HD, 2)])
    perm_q = (np.arange(ATT_HEADS)[:, None] * ATT_HD + perm[None, :]).reshape(-1)
    perm_k = (np.arange(ATT_KV_HEADS)[:, None] * ATT_HD + perm[None, :]).reshape(-1)
    big = jnp.concatenate([
        cols["ml_k"] * (ML_HD ** -0.5), cols["ml_v"], cols["ssm_xbc"],
        cols["att_k"][:, perm_k], cols["att_v"], cols["att_q"][:, perm_q], cols["att_z"],
        cols["ml_q"], cols["ml_o"], cols["ml_z"], cols["ssm_z"], w_gate], axis=1)
    mg = cols["ml_gates"].reshape(-1, 4, ML_HEADS)
    zpad = lambda w: jnp.zeros((w_in.shape[0], w), BF16)
    small = jnp.concatenate([
        mg[:, 0], mg[:, 2], zpad(LANES - 2 * ML_HEADS),
        mg[:, 1], mg[:, 3], zpad(LANES - 2 * ML_HEADS),
        cols["ssm_dt"], zpad(LANES - 2 * SSM_HEADS)], axis=1)
    return big, small


def _rope_tables(seq):
    rows = seq // GRID_W
    row = jnp.repeat(jnp.arange(rows), GRID_W).astype(F32)
    col = jnp.tile(jnp.arange(GRID_W), rows).astype(F32)
    n_freq = ATT_HD // 4
    inv = ROPE_BASE ** (-jnp.arange(n_freq, dtype=F32) / n_freq)
    ang = jnp.concatenate([row[:, None] * inv, col[:, None] * inv], -1)
    cos2 = jnp.concatenate([jnp.cos(ang), jnp.cos(ang)], -1)
    sin2 = jnp.concatenate([-jnp.sin(ang), jnp.sin(ang)], -1)
    q_scale = ATT_SCALE * LOG2E
    return cos2 * q_scale, sin2 * q_scale, cos2, sin2


def _lane_row(v):
    v = v.astype(F32).reshape(1, -1)
    return jnp.pad(v, ((0, 0), (0, LANES - v.shape[1])))


def _ssd_params(dt_bias, a_log, d_skip, norm_g):
    chan_head = np.arange(BRANCH_W) // SSM_HD
    e = np.zeros((2, LANES, BRANCH_W), np.float32)
    for d in range(2):
        e[d, SSM_HEADS * d + chan_head, np.arange(BRANCH_W)] = 1.0
    even = ((chan_head % 2) == 0).astype(np.float32).reshape(1, BRANCH_W)
    dsk = jnp.repeat(d_skip.astype(F32), SSM_HD).reshape(1, BRANCH_W)
    return (_lane_row(dt_bias), _lane_row(-jnp.exp(a_log.astype(F32))), jnp.asarray(e, BF16),
            jnp.asarray(even), dsk, norm_g.astype(F32).reshape(1, BRANCH_W))


def kernel(x, c, ctx, c_ctx, w_mod, b_mod, g_pre, w_in, w_gate, b_gate, att_sink, ml_gate_bias,
           ml_norm, ssm_conv_w, ssm_conv_b, ssm_dt_bias, ssm_a_log, ssm_d, ssm_norm, w_branch,
           w_out, g_post):
    b, s, d = x.shape
    lc = ctx.shape[1]
    depth = w_mod.shape[0]
    rows = -(-(b + 1) // 8) * 8
    cc = jnp.concatenate([c, c_ctx[None, :], jnp.zeros((rows - b - 1, d), F32)], axis=0)
    mod = _modulation(cc, w_mod, b_mod)
    tables = _rope_tables(s)
    xc = ctx.reshape(1, b * lc, d)
    for l in range(depth):
        need_ctx = l < depth - 1
        w_big, w_small = _pack_in_weights(w_in[l].astype(BF16), w_gate[l].astype(BF16))
        mod_x = mod[l, :b].reshape(b, 3, d)
        mod_c = mod[l, b:b + 1].reshape(1, 3, d)
        u, us = _in_proj(x, mod_x, g_pre[l], w_big, w_small)
        uc, usc = _in_proj(xc, mod_c, g_pre[l], w_big, w_small,
                           U_COLS if need_ctx else U_STATE_COLS)
        uc = uc.reshape(b, lc, -1)
        usc = usc.reshape(b, lc, -1)

        sink = att_sink[l].astype(F32)
        o_a = _attention(u, uc, sink, tables)

        gb = ml_gate_bias[l]
        gbias = jnp.concatenate([_lane_row(jnp.concatenate([gb[0], gb[2]])),
                                 _lane_row(jnp.concatenate([gb[1], gb[3]]))], axis=1)
        st0 = (jnp.zeros((b, 2 * ML_HEADS, ML_HD, ML_CW), F32), jnp.zeros((b, 8, LANES), F32))
        sp = _ssd_params(ssm_dt_bias[l], ssm_a_log[l], ssm_d[l], ssm_norm[l])
        h0 = jnp.zeros((b, 2 * SSM_GROUPS, SSM_STATE, SSM_HPG * SSM_HD), F32)
        xbc_c = _ssm_conv(uc, ssm_conv_w[l], ssm_conv_b[l])
        if need_ctx:
            o_m_c, st = _mlstm(uc, usc, gbias, ml_norm[l].astype(F32), st0)
            o_s_c, hst = _ssd(xbc_c, uc, usc, sp, h0)
        else:
            st = _mlstm_state(uc, usc, gbias, st0)
            hst = _ssd_state(xbc_c, usc, sp, h0)
        o_m, _ = _mlstm(u, us, gbias, ml_norm[l].astype(F32), st)
        o_s, _ = _ssd(_ssm_conv(u, ssm_conv_w[l], ssm_conv_b[l]), u, us, sp, hst)

        wb = w_branch[l].astype(BF16)
        bg = b_gate[l].astype(F32).reshape(3, d)
        wo = w_out[l].astype(BF16)
        mixed = _branch_mix(o_a, o_m, o_s, u, wb, bg)
        x_new = _out_proj(mixed, wo, x, mod_x, g_post[l])
        if need_ctx:
            o_a_c = _ctx_attention(uc, sink)
            flat = lambda a: a.reshape(1, b * lc, a.shape[-1])
            mixed_c = _branch_mix(flat(o_a_c), flat(o_m_c), flat(o_s_c), flat(uc), wb, bg)
            xc = _out_proj(mixed_c, wo, xc, mod_c, g_post[l])
        x = x_new
    return x
```

```python
import functools

import numpy as np
import jax
import jax.numpy as jnp
from jax import lax
from jax.experimental import pallas as pl
from jax.experimental.pallas import tpu as pltpu

F32 = jnp.float32
BF16 = jnp.bfloat16
LOG2E = 1.4426950408889634

D_MODEL = 2048
EPS = 1e-6
BRANCH_W = D_MODEL // 2
GRID_W = 64
ROPE_BASE = 10000.0

ATT_HD = 128
ATT_HEADS = BRANCH_W // ATT_HD
ATT_KV_HEADS = ATT_HEADS // 4
ATT_REP = ATT_HEADS // ATT_KV_HEADS
ATT_KV_W = ATT_KV_HEADS * ATT_HD
ATT_BLOCK = 128
ATT_SCALE = ATT_HD ** -0.5

ML_HEADS = 4
ML_HD = BRANCH_W // ML_HEADS

SSM_HD = 64
SSM_HEADS = BRANCH_W // SSM_HD
SSM_GROUPS = 2
SSM_HPG = SSM_HEADS // SSM_GROUPS
SSM_STATE = 128
SSM_BC_W = SSM_GROUPS * SSM_STATE
SSM_CONV = 5
SSM_CONV_CH = BRANCH_W + 2 * SSM_BC_W

LANES = 128
CHUNK = LANES
NEG = -1e30
VMEM_LIMIT = 56 * 1024 * 1024

U_ML_K = 0
U_ML_V = 1024
U_SSM_X = 2048
U_SSM_B = 3072
U_SSM_C = 3328
U_ATT_K = 3584
U_ATT_V = 3840
U_ATT_Q = 4096
U_ATT_Z = 5120
U_ML_Q = 6144
U_ML_O = 7168
U_ML_Z = 8192
U_SSM_Z = 9216
U_GATE = 10240
U_COLS = U_GATE + 3 * D_MODEL
S_ML_I = 0
S_ML_F = LANES
S_SSM_DT = 2 * LANES
S_COLS = 3 * LANES
ML_CW = ML_HD + LANES

IN_TN = 1024
U_SILU_TILES = (U_ATT_Z // IN_TN, U_ML_Z // IN_TN, U_SSM_Z // IN_TN)
U_SIGMOID_TILE = U_ML_O // IN_TN
U_STATE_COLS = U_ATT_Q


def _cparams(*sem):
    return pltpu.CompilerParams(dimension_semantics=sem, vmem_limit_bytes=VMEM_LIMIT)


def _sigmoid(v):
    return 0.5 * jnp.tanh(0.5 * v) + 0.5


def _silu(v):
    return v * _sigmoid(v)


def _softplus(v):
    return jnp.maximum(v, 0.0) + jnp.log(1.0 + jnp.exp(-jnp.abs(v)))


def _log_sigmoid(v):
    return jnp.minimum(v, 0.0) - jnp.log(1.0 + jnp.exp(-jnp.abs(v)))


def _dot(a, b):
    return jnp.dot(a, b, preferred_element_type=F32)


def _dot_nt(a, b):
    return lax.dot_general(a, b, (((1,), (1,)), ((), ())), preferred_element_type=F32)


def _dot_tn(a, b):
    return lax.dot_general(a, b, (((0,), (0,)), ((), ())), preferred_element_type=F32)


def _scan_cumsum(tri_bf, v):
    hi = v.astype(BF16)
    r1 = v - hi.astype(F32)
    mid = r1.astype(BF16)
    lo = (r1 - mid.astype(F32)).astype(BF16)
    res = _dot(tri_bf, jnp.concatenate([hi, mid, lo], axis=1))
    return res[:, :LANES] + res[:, LANES:2 * LANES] + res[:, 2 * LANES:]


def _expand2(v, e_mat):
    hi = v.astype(BF16)
    lo = (v - hi.astype(F32)).astype(BF16)
    return _dot(hi, e_mat) + _dot(lo, e_mat)


def _mod_kernel(c_ref, w_ref, b_ref, o_ref):
    sc = _silu(c_ref[...]).astype(BF16)
    o_ref[0] = _dot(sc, w_ref[0].astype(BF16)) + b_ref[0]


def _modulation(cc, w_mod, b_mod):
    depth, d, n = w_mod.shape
    rows = cc.shape[0]
    tn = 1024
    return pl.pallas_call(
        _mod_kernel,
        grid=(depth, n // tn),
        in_specs=[pl.BlockSpec((rows, d), lambda l, j: (0, 0)),
                  pl.BlockSpec((1, d, tn), lambda l, j: (l, 0, j)),
                  pl.BlockSpec((1, 1, tn), lambda l, j: (l, 0, j))],
        out_specs=pl.BlockSpec((1, rows, tn), lambda l, j: (l, 0, j)),
        out_shape=jax.ShapeDtypeStruct((depth, rows, n), F32),
        compiler_params=_cparams("arbitrary", "arbitrary"),
        name="modulation",
    )(cc, w_mod, b_mod.reshape(depth, 1, n))


def _in_proj_kernel(x_ref, mod_ref, g_ref, w_ref, ws_ref, u_ref, us_ref, h_scr):
    j = pl.program_id(2)

    @pl.when(j == 0)
    def _():
        x = x_ref[0]
        ms = jnp.mean(x * x, axis=-1, keepdims=True)
        xn = x * lax.rsqrt(ms + EPS) * g_ref[...]
        h = xn * (1.0 + mod_ref[0, 1:2, :]) + mod_ref[0, 0:1, :]
        hb = h.astype(BF16)
        h_scr[...] = hb
        us_ref[0] = _dot(hb, ws_ref[...])

    is_silu = functools.reduce(jnp.logical_or, [j == t for t in U_SILU_TILES])
    is_sigmoid = j == U_SIGMOID_TILE

    @pl.when(is_silu)
    def _():
        u_ref[0] = _silu(_dot(h_scr[...], w_ref[...])).astype(BF16)

    @pl.when(is_sigmoid)
    def _():
        u_ref[0] = _sigmoid(_dot(h_scr[...], w_ref[...])).astype(BF16)

    @pl.when(jnp.logical_not(jnp.logical_or(is_silu, is_sigmoid)))
    def _():
        u_ref[0] = _dot(h_scr[...], w_ref[...]).astype(BF16)


def _in_proj(x3, mod3, g_pre, w_big, w_small, n=U_COLS):
    g, m, d = x3.shape
    tm = min(1024, m)
    return pl.pallas_call(
        _in_proj_kernel,
        grid=(g, m // tm, n // IN_TN),
        in_specs=[pl.BlockSpec((1, tm, d), lambda b, i, j: (b, i, 0)),
                  pl.BlockSpec((1, 3, d), lambda b, i, j: (b, 0, 0)),
                  pl.BlockSpec((1, d), lambda b, i, j: (0, 0)),
                  pl.BlockSpec((d, IN_TN), lambda b, i, j: (0, j)),
                  pl.BlockSpec((d, S_COLS), lambda b, i, j: (0, 0))],
        out_specs=[pl.BlockSpec((1, tm, IN_TN), lambda b, i, j: (b, i, j)),
                   pl.BlockSpec((1, tm, S_COLS), lambda b, i, j: (b, i, 0))],
        out_shape=[jax.ShapeDtypeStruct((g, m, n), BF16),
                   jax.ShapeDtypeStruct((g, m, S_COLS), F32)],
        scratch_shapes=[pltpu.VMEM((tm, d), BF16)],
        compiler_params=_cparams("arbitrary", "arbitrary", "arbitrary"),
        name="in_proj",
    )(x3, mod3, g_pre.reshape(1, d), w_big, w_small)


def _att_kernel(sink_ref, q_ref, z_ref, kp_ref, kc_ref, kn_ref, vp_ref, vc_ref, vn_ref,
                ctxk_ref, ctxv_ref, cq_ref, sq_ref, ck_ref, sk_ref, o_ref, *, nb):
    i = pl.program_id(1)
    blk = ATT_BLOCK
    r0 = pl.multiple_of(i * blk, blk)
    rp = pl.multiple_of(jnp.maximum(i - 1, 0) * blk, blk)
    rn = pl.multiple_of(jnp.minimum(i + 1, nb - 1) * blk, blk)

    def rope(t, cos2, sin2):
        return t * cos2 + pltpu.roll(t, ATT_HD // 2, 1) * sin2

    cq = cq_ref[pl.ds(r0, blk), :]
    sq = sq_ref[pl.ds(r0, blk), :]
    key = lax.broadcasted_iota(jnp.int32, (blk, blk), 0)
    qry = lax.broadcasted_iota(jnp.int32, (blk, blk), 1)
    bias_p = jnp.where(key >= qry, 0.0, NEG) + jnp.where(i > 0, 0.0, NEG)
    bias_n = jnp.where(key <= qry, 0.0, NEG) + jnp.where(i < nb - 1, 0.0, NEG)
    bias_p = jnp.concatenate([bias_p, bias_p], axis=1)
    bias_n = jnp.concatenate([bias_n, bias_n], axis=1)
    scores = []
    for g in range(ATT_KV_HEADS):
        ks = slice(g * ATT_HD, (g + 1) * ATT_HD)
        kparts = []
        for k_ref, r in ((kp_ref, rp), (kc_ref, r0), (kn_ref, rn)):
            kr = rope(k_ref[0][:, ks].astype(F32), ck_ref[pl.ds(r, blk), :], sk_ref[pl.ds(r, blk), :])
            kparts.append(kr.astype(BF16))
        kk = jnp.concatenate(kparts + [ctxk_ref[0][:, ks]], axis=0)
        vv = jnp.concatenate([vp_ref[0][:, ks], vc_ref[0][:, ks], vn_ref[0][:, ks],
                              ctxv_ref[0][:, ks]], axis=0)
        vv_t = vv.T
        for pair in range(ATT_REP // 2):
            heads = [g * ATT_REP + 2 * pair + j for j in range(2)]
            q2 = jnp.concatenate(
                [rope(q_ref[0][:, hh * ATT_HD:(hh + 1) * ATT_HD].astype(F32), cq, sq).astype(BF16)
                 for hh in heads], axis=0)
            scores.append((heads, vv_t, _dot_nt(kk, q2)))
    for heads, vv_t, s_t in scores:
        sink = jnp.concatenate([jnp.full((1, blk), sink_ref[hh] * LOG2E, F32) for hh in heads],
                               axis=1)
        s_t = jnp.concatenate([s_t[:blk] + bias_p, s_t[blk:2 * blk],
                               s_t[2 * blk:3 * blk] + bias_n, s_t[3 * blk:]], axis=0)
        m = jnp.maximum(jnp.max(s_t, axis=0, keepdims=True), sink)
        e_t = jnp.exp2(s_t - m)
        den = jnp.sum(e_t, axis=0, keepdims=True) + jnp.exp2(sink - m)
        o_t = _dot(vv_t, e_t.astype(BF16)) / den
        for j, hh in enumerate(heads):
            hs = slice(hh * ATT_HD, (hh + 1) * ATT_HD)
            o = o_t[:, j * blk:(j + 1) * blk].T
            o_ref[0, :, hs] = (o * z_ref[0][:, hs].astype(F32)).astype(BF16)


def _attention(u, uc, sink, tables):
    b, s, _ = u.shape
    lc = uc.shape[1]
    nb = s // ATT_BLOCK
    blk = ATT_BLOCK
    kcol = U_ATT_K // ATT_KV_W
    vcol = U_ATT_V // ATT_KV_W

    def kv_spec(col, shift):
        return pl.BlockSpec((1, blk, ATT_KV_W),
                            lambda bb, i: (bb, jnp.clip(i + shift, 0, nb - 1), col))

    tab_spec = pl.BlockSpec((s, ATT_HD), lambda bb, i: (0, 0))
    return pl.pallas_call(
        functools.partial(_att_kernel, nb=nb),
        grid=(b, nb),
        in_specs=[pl.BlockSpec(memory_space=pltpu.SMEM),
                  pl.BlockSpec((1, blk, BRANCH_W), lambda bb, i: (bb, i, U_ATT_Q // BRANCH_W)),
                  pl.BlockSpec((1, blk, BRANCH_W), lambda bb, i: (bb, i, U_ATT_Z // BRANCH_W)),
                  kv_spec(kcol, -1), kv_spec(kcol, 0), kv_spec(kcol, 1),
                  kv_spec(vcol, -1), kv_spec(vcol, 0), kv_spec(vcol, 1),
                  pl.BlockSpec((1, lc, ATT_KV_W), lambda bb, i: (bb, 0, kcol)),
                  pl.BlockSpec((1, lc, ATT_KV_W), lambda bb, i: (bb, 0, vcol)),
                  tab_spec, tab_spec, tab_spec, tab_spec],
        out_specs=pl.BlockSpec((1, blk, BRANCH_W), lambda bb, i: (bb, i, 0)),
        out_shape=jax.ShapeDtypeStruct((b, s, BRANCH_W), BF16),
        compiler_params=_cparams("arbitrary", "arbitrary"),
        name="window_attention",
    )(sink, u, u, u, u, u, u, u, u, uc, uc, *tables)


def _ctx_att_kernel(sink_ref, q_ref, z_ref, k_ref, v_ref, o_ref):
    for hh in range(ATT_HEADS):
        g = hh // ATT_REP
        ks = slice(g * ATT_HD, (g + 1) * ATT_HD)
        hs = slice(hh * ATT_HD, (hh + 1) * ATT_HD)
        qh = (q_ref[0][:, hs].astype(F32) * ATT_SCALE).astype(BF16)
        sink = sink_ref[hh]
        s = _dot_nt(qh, k_ref[0][:, ks])
        m = jnp.maximum(jnp.max(s, axis=1, keepdims=True), sink)
        e = jnp.exp(s - m)
        den = jnp.sum(e, axis=1, keepdims=True) + jnp.exp(sink - m)
        o = _dot(e.astype(BF16), v_ref[0][:, ks]) / den
        o_ref[0, :, hs] = (o * z_ref[0][:, hs].astype(F32)).astype(BF16)


def _ctx_attention(uc, sink):
    b, lc, _ = uc.shape
    kcol = U_ATT_K // ATT_KV_W
    vcol = U_ATT_V // ATT_KV_W
    return pl.pallas_call(
        _ctx_att_kernel,
        grid=(b,),
        in_specs=[pl.BlockSpec(memory_space=pltpu.SMEM),
                  pl.BlockSpec((1, lc, BRANCH_W), lambda bb: (bb, 0, U_ATT_Q // BRANCH_W)),
                  pl.BlockSpec((1, lc, BRANCH_W), lambda bb: (bb, 0, U_ATT_Z // BRANCH_W)),
                  pl.BlockSpec((1, lc, ATT_KV_W), lambda bb: (bb, 0, kcol)),
                  pl.BlockSpec((1, lc, ATT_KV_W), lambda bb: (bb, 0, vcol))],
        out_specs=pl.BlockSpec((1, lc, BRANCH_W), lambda bb: (bb, 0, 0)),
        out_shape=jax.ShapeDtypeStruct((b, lc, BRANCH_W), BF16),
        compiler_params=_cparams("arbitrary"),
        name="context_attention",
    )(sink, uc, uc, uc, uc)


def _scan_masks(reverse, n):
    row = lax.broadcasted_iota(jnp.int32, (n, n), 0)
    col = lax.broadcasted_iota(jnp.int32, (n, n), 1)
    tri = (col >= row) if reverse else (col <= row)
    return tri, jnp.where(tri, 1.0, 0.0).astype(BF16)


def _cummax_rows(a, reverse):
    n = a.shape[0]
    row = lax.broadcasted_iota(jnp.int32, a.shape, 0)
    k = 1
    while k < n:
        if reverse:
            a = jnp.maximum(a, jnp.where(row < n - k, pltpu.roll(a, n - k, 0), NEG))
        else:
            a = jnp.maximum(a, jnp.where(row >= k, pltpu.roll(a, k, 0), NEG))
        k *= 2
    return a


SKEW_WIDTH = 8


def _run_skewed(gens):
    pending = list(gens)
    active = []
    while pending or active:
        for _ in range(SKEW_WIDTH):
            if pending:
                active.append(pending.pop(0))
        for gen in list(active):
            if next(gen, StopIteration) is StopIteration:
                active.remove(gen)


def _mlstm_gates(gates, reverse, g_ref, gbias_ref, m_row):
    n = CHUNK
    last = 0 if reverse else n - 1
    tri, trif = _scan_masks(reverse, n)
    g_all = g_ref[0] + gbias_ref[...]
    b_all = _scan_cumsum(trif, _log_sigmoid(g_all[:, S_ML_F:S_ML_F + LANES]))
    yield
    a_all = g_all[:, S_ML_I:S_ML_I + LANES] - b_all
    mx_all = jnp.maximum(_cummax_rows(a_all, reverse), m_row)
    yield
    mt_all = b_all + mx_all
    gates.update(tri=tri, mx=mx_all, fl=jnp.exp(-mt_all), mx_last=mx_all[last:last + 1, :],
                 a_t=a_all.T, m_new=mt_all[last:last + 1, :])


def _mlstm_head(gates, ln, hs, q_ref, k_ref, v_ref, c_ref, m_row, outs):
    n = CHUNK
    a_row = gates["a_t"][ln:ln + 1, :]
    m_prev = m_row[:, ln:ln + 1]
    mxl = gates["mx_last"][:, ln:ln + 1]
    k_t = k_ref[0][:, hs].T
    v1 = jnp.concatenate([v_ref[0][:, hs], jnp.ones((n, LANES), BF16)], axis=1)
    c_prev = c_ref[0, ln]
    if q_ref is not None:
        qh = q_ref[0][:, hs]
        s = _dot(qh, k_t)
        qc = _dot(qh, c_prev.astype(BF16))
    yield
    kw_t = (k_t.astype(F32) * jnp.exp(a_row - mxl)).astype(BF16)
    upd = _dot(kw_t, v1)
    yield
    if q_ref is not None:
        mx_b = jnp.broadcast_to(gates["mx"][:, ln:ln + 1], (n, LANES))
        w = jnp.exp(jnp.where(gates["tri"], a_row - mx_b, NEG))
        pv = _dot((s * w).astype(BF16), v1)
        yield
    c_ref[0, ln] = jnp.exp(m_prev - mxl) * c_prev + upd
    if q_ref is not None:
        yield
        fl_b = jnp.broadcast_to(gates["fl"][:, ln:ln + 1], (n, LANES))
        e_int = jnp.exp(m_prev - mx_b)
        nd = jnp.concatenate([e_int] * (ML_CW // LANES), axis=1) * qc + pv
        r = 1.0 / jnp.maximum(jnp.abs(nd[:, ML_HD:]), fl_b)
        outs[ln] = nd[:, :ML_HD] * jnp.concatenate([r] * (ML_HD // LANES), axis=1)


def _mlstm_step(fwd_refs, bwd_refs, gbias_ref, c_ref, m_row):
    gates = [{}, {}]
    _run_skewed([_mlstm_gates(gates[d], bool(d), refs[3], gbias_ref, m_row)
                 for d, refs in enumerate((fwd_refs, bwd_refs))])
    outs = {}
    gens = []
    for h in range(ML_HEADS):
        for d, refs in enumerate((fwd_refs, bwd_refs)):
            gens.append(_mlstm_head(gates[d], d * ML_HEADS + h, slice(h * ML_HD, (h + 1) * ML_HD),
                                    refs[0], refs[1], refs[2], c_ref, m_row, outs))
    _run_skewed(gens)
    lane = lax.broadcasted_iota(jnp.int32, (1, LANES), 1)
    m_new = jnp.where(lane < ML_HEADS, gates[0]["m_new"], gates[1]["m_new"])
    if fwd_refs[0] is None:
        return None, None, m_new
    h_f = jnp.concatenate([outs[h] for h in range(ML_HEADS)], axis=1)
    h_b = jnp.concatenate([outs[ML_HEADS + h] for h in range(ML_HEADS)], axis=1)
    return h_f, h_b, m_new


def _mlstm_finish(h_sum, o_ref, z_ref, ng_ref):
    t = o_ref[0].astype(F32) * h_sum
    parts = []
    for h in range(ML_HEADS):
        th = t[:, h * ML_HD:(h + 1) * ML_HD]
        parts.append(th * lax.rsqrt(jnp.mean(th * th, axis=1, keepdims=True) + EPS))
    hn = jnp.concatenate(parts, axis=1)
    return (hn * ng_ref[...] * z_ref[0].astype(F32)).astype(BF16)


def _mlstm_kernel(qf, kf, vf, gf, of, zf, qb, kb, vb, gb, ob, zb, gbias, ng, c0, m0,
                  out, cfin, mfin, h_scr, *, nc):
    s = pl.program_id(1)

    @pl.when(s == 0)
    def _():
        cfin[...] = c0[...]
        mfin[...] = m0[...]

    m_row = mfin[0, 0:1, :]
    h_f, h_b, m_new = _mlstm_step((qf, kf, vf, gf), (qb, kb, vb, gb), gbias, cfin, m_row)
    mfin[0] = jnp.broadcast_to(m_new, (8, LANES))
    rf = pl.multiple_of(s * CHUNK, CHUNK)
    rb = pl.multiple_of((nc - 1 - s) * CHUNK, CHUNK)

    @pl.when(s < nc // 2)
    def _():
        h_scr[pl.ds(rf, CHUNK), :] = h_f
        h_scr[pl.ds(rb, CHUNK), :] = h_b

    @pl.when(s >= nc // 2)
    def _():
        out[0, pl.ds(rf, CHUNK), :] = _mlstm_finish(h_f + h_scr[pl.ds(rf, CHUNK), :], of, zf, ng)
        out[0, pl.ds(rb, CHUNK), :] = _mlstm_finish(h_b + h_scr[pl.ds(rb, CHUNK), :], ob, zb, ng)


def _mlstm_state_kernel(kf, vf, gf, kb, vb, gb, gbias, c0, m0, cfin, mfin):
    @pl.when(pl.program_id(1) == 0)
    def _():
        cfin[...] = c0[...]
        mfin[...] = m0[...]

    m_row = mfin[0, 0:1, :]
    _, _, m_new = _mlstm_step((None, kf, vf, gf), (None, kb, vb, gb), gbias, cfin, m_row)
    mfin[0] = jnp.broadcast_to(m_new, (8, LANES))


def _mlstm_state(u, us, gbias_row, state):
    b, t, _ = u.shape
    nc = t // CHUNK
    c0, m0 = state
    gw = 2 * LANES

    def chunk(width, col, rev):
        if rev:
            return pl.BlockSpec((1, CHUNK, width), lambda bb, s: (bb, nc - 1 - s, col))
        return pl.BlockSpec((1, CHUNK, width), lambda bb, s: (bb, s, col))

    def dir_specs(rev):
        return [chunk(BRANCH_W, U_ML_K // BRANCH_W, rev), chunk(BRANCH_W, U_ML_V // BRANCH_W, rev),
                chunk(gw, 0, rev)]

    c_spec = pl.BlockSpec((1, 2 * ML_HEADS, ML_HD, ML_CW), lambda bb, s: (bb, 0, 0, 0))
    m_spec = pl.BlockSpec((1, 8, LANES), lambda bb, s: (bb, 0, 0))
    cf, mf = pl.pallas_call(
        _mlstm_state_kernel,
        grid=(b, nc),
        in_specs=dir_specs(False) + dir_specs(True) + [
            pl.BlockSpec((1, gw), lambda bb, s: (0, 0)), c_spec, m_spec],
        out_specs=[c_spec, m_spec],
        out_shape=[jax.ShapeDtypeStruct(c0.shape, F32), jax.ShapeDtypeStruct(m0.shape, F32)],
        compiler_params=_cparams("arbitrary", "arbitrary"),
        name="mlstm_state",
    )(u, u, us, u, u, us, gbias_row, c0, m0)
    return cf, mf


def _mlstm(u, us, gbias_row, norm_g, state):
    b, t, _ = u.shape
    nc = t // CHUNK
    assert nc % 2 == 0
    c0, m0 = state
    nst = 2 * ML_HEADS
    gw = 2 * LANES

    def fwd(col):
        return pl.BlockSpec((1, CHUNK, BRANCH_W), lambda bb, s: (bb, s, col // BRANCH_W))

    def bwd(col):
        return pl.BlockSpec((1, CHUNK, BRANCH_W), lambda bb, s: (bb, nc - 1 - s, col // BRANCH_W))

    gate_f = pl.BlockSpec((1, CHUNK, gw), lambda bb, s: (bb, s, 0))
    gate_b = pl.BlockSpec((1, CHUNK, gw), lambda bb, s: (bb, nc - 1 - s, 0))
    c_spec = pl.BlockSpec((1, nst, ML_HD, ML_CW), lambda bb, s: (bb, 0, 0, 0))
    m_spec = pl.BlockSpec((1, 8, LANES), lambda bb, s: (bb, 0, 0))
    out, cf, mf = pl.pallas_call(
        functools.partial(_mlstm_kernel, nc=nc),
        grid=(b, nc),
        in_specs=[fwd(U_ML_Q), fwd(U_ML_K), fwd(U_ML_V), gate_f, fwd(U_ML_O), fwd(U_ML_Z),
                  bwd(U_ML_Q), bwd(U_ML_K), bwd(U_ML_V), gate_b, bwd(U_ML_O), bwd(U_ML_Z),
                  pl.BlockSpec((1, gw), lambda bb, s: (0, 0)),
                  pl.BlockSpec((1, BRANCH_W), lambda bb, s: (0, 0)),
                  c_spec, m_spec],
        out_specs=[pl.BlockSpec((1, t, BRANCH_W), lambda bb, s: (bb, 0, 0)), c_spec, m_spec],
        out_shape=[jax.ShapeDtypeStruct((b, t, BRANCH_W), BF16),
                   jax.ShapeDtypeStruct(c0.shape, F32),
                   jax.ShapeDtypeStruct(m0.shape, F32)],
        scratch_shapes=[pltpu.VMEM((t, BRANCH_W), F32)],
        compiler_params=_cparams("arbitrary", "arbitrary"),
        name="mlstm",
    )(u, u, u, us, u, u, u, u, u, us, u, u, gbias_row, norm_g.reshape(1, BRANCH_W), c0, m0)
    return out, (cf, mf)


CONV_TN = 512
CONV_PAD = 16


def _conv_kernel(x_ref, w_ref, b_ref, o_ref, pad_scr, *, t):
    pad_scr[0:CONV_PAD, :] = jnp.zeros((CONV_PAD, CONV_TN), F32)
    pad_scr[CONV_PAD + t:2 * CONV_PAD + t, :] = jnp.zeros((CONV_PAD, CONV_TN), F32)
    pad_scr[CONV_PAD:CONV_PAD + t, :] = x_ref[0].astype(F32)
    rows = 256
    half = SSM_CONV // 2
    halo = 8
    for c in range(t // rows):
        r0 = c * rows
        win = pad_scr[CONV_PAD + r0 - halo:CONV_PAD + r0 + rows + halo, :]
        acc = jnp.broadcast_to(b_ref[...], (rows, CONV_TN))
        for k in range(SSM_CONV):
            sh = win if k == half else pltpu.roll(win, (half - k) % (rows + 2 * halo), 0)
            acc = acc + w_ref[k:k + 1, :] * sh[halo:halo + rows, :]
        o_ref[0, r0:r0 + rows, :] = _silu(acc).astype(BF16)


def _ssm_conv(u, conv_w, conv_b):
    b, t, _ = u.shape
    w8 = jnp.zeros((8, SSM_CONV_CH), F32).at[:SSM_CONV].set(conv_w)
    first = U_SSM_X // CONV_TN
    return pl.pallas_call(
        functools.partial(_conv_kernel, t=t),
        grid=(b, SSM_CONV_CH // CONV_TN),
        in_specs=[pl.BlockSpec((1, t, CONV_TN), lambda bb, j: (bb, 0, first + j)),
                  pl.BlockSpec((8, CONV_TN), lambda bb, j: (0, j)),
                  pl.BlockSpec((1, CONV_TN), lambda bb, j: (0, j))],
        out_specs=pl.BlockSpec((1, t, CONV_TN), lambda bb, j: (bb, 0, j)),
        out_shape=jax.ShapeDtypeStruct((b, t, SSM_CONV_CH), BF16),
        scratch_shapes=[pltpu.VMEM((t + 2 * CONV_PAD, CONV_TN), F32)],
        compiler_params=_cparams("arbitrary", "arbitrary"),
        name="ssm_conv",
    )(u, w8, conv_b.reshape(1, SSM_CONV_CH))


def _ssd_dir(pro, reverse, x_ref, bm_ref, cm_ref, dt_ref, dtb_ref, a_ref, e_ref, even_ref, h_ref):
    d = 1 if reverse else 0
    n = CHUNK
    last = 0 if reverse else n - 1
    tri, trif = _scan_masks(reverse, n)
    dt_all = _softplus(dt_ref[0] + dtb_ref[...])
    cum = _scan_cumsum(trif, dt_all * a_ref[...])
    yield
    total = cum[last:last + 1, :]
    cum2 = cum * LOG2E
    src2_t = (cum2 - jnp.log2(dt_all)).T
    e_mat = e_ref[d]
    wd_x = _dot((jnp.exp(total - cum) * dt_all).astype(BF16), e_mat)
    tot_x = _expand2(jnp.broadcast_to(jnp.exp(total), (8, LANES)), e_mat)[0:1, :]
    yield
    x = x_ref[0]
    pro.update(tri=tri, cum2=cum2, src2_t=src2_t, x=x, d=d, e_mat=e_mat, tot_x=tot_x,
               xw=(x.astype(F32) * wd_x).astype(BF16), bm_ref=bm_ref, cm_ref=cm_ref, h_ref=h_ref)
    if cm_ref is not None:
        even = even_ref[...].astype(BF16)
        pro["x_par"] = (x * even, x * (1.0 - even))
        pro["dec"] = jnp.exp(cum)
        yield
        pro["cb"] = [_dot_nt(cm_ref[0][:, g * SSM_STATE:(g + 1) * SSM_STATE],
                             bm_ref[0][:, g * SSM_STATE:(g + 1) * SSM_STATE])
                     for g in range(SSM_GROUPS)]


def _ssd_head(pro, g, p, par, terms):
    n = CHUNK
    ln = SSM_HEADS * pro["d"] + g * SSM_HPG + 2 * p + par
    lo = (g * SSM_HPG + 2 * p) * SSM_HD
    c_b = jnp.broadcast_to(pro["cum2"][:, ln:ln + 1], (n, LANES))
    seg = jnp.where(pro["tri"], c_b - pro["src2_t"][ln:ln + 1, :], NEG)
    yield
    mix = (pro["cb"][g] * jnp.exp2(seg)).astype(BF16)
    yield
    term = _dot(mix, pro["x_par"][par][:, lo:lo + 2 * SSM_HD])
    key = (pro["d"], g, p)
    terms[key] = term if par == 0 else terms[key] + term


def _ssd_group_end(pro, g, terms, ys):
    yield
    yield
    gw = SSM_HPG * SSM_HD
    gs = slice(g * gw, (g + 1) * gw)
    d = pro["d"]
    ss = slice(g * SSM_STATE, (g + 1) * SSM_STATE)
    h_prev = pro["h_ref"][0, d * SSM_GROUPS + g]
    if pro["cm_ref"] is not None:
        inter = _dot(pro["cm_ref"][0][:, ss], h_prev.astype(BF16))
        dec_x = _dot(pro["dec"].astype(BF16), pro["e_mat"][:, gs])
        ys[(d, g)] = (jnp.concatenate([terms[(d, g, p)] for p in range(SSM_HPG // 2)], axis=1)
                      + dec_x * inter)
    pro["h_ref"][0, d * SSM_GROUPS + g] = (pro["tot_x"][:, gs] * h_prev
                                           + _dot_tn(pro["bm_ref"][0][:, ss], pro["xw"][:, gs]))
    yield


def _ssd_step(fwd_refs, bwd_refs, dtb_ref, a_ref, e_ref, even_ref, h_ref):
    pros = [{}, {}]
    _run_skewed([_ssd_dir(pros[d], bool(d), refs[0], refs[1], refs[2], refs[3], dtb_ref, a_ref, e_ref,
                          even_ref, h_ref) for d, refs in enumerate((fwd_refs, bwd_refs))])
    with_out = fwd_refs[2] is not None
    terms, ys, gens = {}, {}, []
    for g in range(SSM_GROUPS):
        if with_out:
            gens += [_ssd_head(pros[d], g, p, par, terms)
                     for p in range(SSM_HPG // 2) for par in range(2) for d in range(2)]
        gens += [_ssd_group_end(pros[d], g, terms, ys) for d in range(2)]
    _run_skewed(gens)
    if not with_out:
        return None
    y_f, y_b = (jnp.concatenate([ys[(d, g)] for g in range(SSM_GROUPS)], axis=1) for d in range(2))
    return y_f, pros[0]["x"], y_b, pros[1]["x"]


def _ssd_finish(y_sum, x, z_ref, dsk_ref, ng_ref):
    y = (y_sum + dsk_ref[...] * x.astype(F32)) * z_ref[0].astype(F32)
    y = y * lax.rsqrt(jnp.mean(y * y, axis=1, keepdims=True) + EPS)
    return (y * ng_ref[...]).astype(BF16)


def _ssd_kernel(xf, bf, cf, sf, zf, xb, bb_, cb_, sb, zb, dtb, arow, emat, even, dsk, ng, h0,
                out, hfin, y_scr, *, nc):
    s = pl.program_id(1)

    @pl.when(s == 0)
    def _():
        hfin[...] = h0[...]

    y_f, x_f, y_b, x_b = _ssd_step((xf, bf, cf, sf), (xb, bb_, cb_, sb), dtb, arow, emat, even, hfin)
    rf = pl.multiple_of(s * CHUNK, CHUNK)
    rb = pl.multiple_of((nc - 1 - s) * CHUNK, CHUNK)

    @pl.when(s < nc // 2)
    def _():
        y_scr[pl.ds(rf, CHUNK), :] = y_f
        y_scr[pl.ds(rb, CHUNK), :] = y_b

    @pl.when(s >= nc // 2)
    def _():
        out[0, pl.ds(rf, CHUNK), :] = _ssd_finish(y_f + y_scr[pl.ds(rf, CHUNK), :], x_f, zf, dsk, ng)
        out[0, pl.ds(rb, CHUNK), :] = _ssd_finish(y_b + y_scr[pl.ds(rb, CHUNK), :], x_b, zb, dsk, ng)


def _ssd_state_kernel(xf, bf, sf, xb, bb_, sb, dtb, arow, emat, h0, hfin):
    @pl.when(pl.program_id(1) == 0)
    def _():
        hfin[...] = h0[...]

    _ssd_step((xf, bf, None, sf), (xb, bb_, None, sb), dtb, arow, emat, None, hfin)


def _ssd_state(xbc, us, params, h0):
    b, t, _ = xbc.shape
    nc = t // CHUNK
    dtb_row, a_row, e_mat = params[:3]

    def chunk(width, col, rev):
        if rev:
            return pl.BlockSpec((1, CHUNK, width), lambda bb, s: (bb, nc - 1 - s, col))
        return pl.BlockSpec((1, CHUNK, width), lambda bb, s: (bb, s, col))

    def dir_specs(rev):
        return [chunk(BRANCH_W, 0, rev), chunk(SSM_BC_W, BRANCH_W // SSM_BC_W, rev),
                chunk(LANES, S_SSM_DT // LANES, rev)]

    row_spec = pl.BlockSpec((1, LANES), lambda bb, s: (0, 0))
    h_spec = pl.BlockSpec((1, 2 * SSM_GROUPS, SSM_STATE, SSM_HPG * SSM_HD),
                          lambda bb, s: (bb, 0, 0, 0))
    return pl.pallas_call(
        _ssd_state_kernel,
        grid=(b, nc),
        in_specs=dir_specs(False) + dir_specs(True) + [
            row_spec, row_spec, pl.BlockSpec((2, LANES, BRANCH_W), lambda bb, s: (0, 0, 0)), h_spec],
        out_specs=h_spec,
        out_shape=jax.ShapeDtypeStruct(h0.shape, F32),
        compiler_params=_cparams("arbitrary", "arbitrary"),
        name="ssd_state",
    )(xbc, xbc, us, xbc, xbc, us, dtb_row, a_row, e_mat, h0)


def _ssd(xbc, u, us, params, h0):
    b, t, _ = u.shape
    nc = t // CHUNK
    assert nc % 2 == 0
    dtb_row, a_row, e_mat, even_row, dsk_row, ng_row = params
    nst = 2 * SSM_GROUPS
    gw = SSM_HPG * SSM_HD

    def chunk_spec(width, col, rev):
        if rev:
            return pl.BlockSpec((1, CHUNK, width), lambda bb, s: (bb, nc - 1 - s, col))
        return pl.BlockSpec((1, CHUNK, width), lambda bb, s: (bb, s, col))

    def dir_specs(rev):
        return [chunk_spec(BRANCH_W, 0, rev),
                chunk_spec(SSM_BC_W, BRANCH_W // SSM_BC_W, rev),
                chunk_spec(SSM_BC_W, BRANCH_W // SSM_BC_W + 1, rev),
                chunk_spec(LANES, S_SSM_DT // LANES, rev),
                chunk_spec(BRANCH_W, U_SSM_Z // BRANCH_W, rev)]

    def row_spec(width):
        return pl.BlockSpec((1, width), lambda bb, s: (0, 0))

    h_spec = pl.BlockSpec((1, nst, SSM_STATE, gw), lambda bb, s: (bb, 0, 0, 0))
    out, hf = pl.pallas_call(
        functools.partial(_ssd_kernel, nc=nc),
        grid=(b, nc),
        in_specs=dir_specs(False) + dir_specs(True) + [
            row_spec(LANES), row_spec(LANES),
            pl.BlockSpec((2, LANES, BRANCH_W), lambda bb, s: (0, 0, 0)),
            row_spec(BRANCH_W), row_spec(BRANCH_W), row_spec(BRANCH_W), h_spec],
        out_specs=[pl.BlockSpec((1, t, BRANCH_W), lambda bb, s: (bb, 0, 0)), h_spec],
        out_shape=[jax.ShapeDtypeStruct((b, t, BRANCH_W), BF16),
                   jax.ShapeDtypeStruct(h0.shape, F32)],
        scratch_shapes=[pltpu.VMEM((t, BRANCH_W), F32)],
        compiler_params=_cparams("arbitrary", "arbitrary"),
        name="ssd",
    )(xbc, xbc, xbc, us, u, xbc, xbc, xbc, us, u,
      dtb_row, a_row, e_mat, even_row, dsk_row, ng_row, h0)
    return out, hf


MIX_TN = 1024


def _branch_mix_kernel(oa_ref, om_ref, os_ref, wb_ref, ga_ref, gm_ref, gs_ref, bg_ref, out_ref):
    acc = None
    for n, (o_ref, g_ref) in enumerate(((oa_ref, ga_ref), (om_ref, gm_ref), (os_ref, gs_ref))):
        gate = _sigmoid(g_ref[0].astype(F32) + bg_ref[n:n + 1, :])
        term = gate * _dot(o_ref[0], wb_ref[n])
        acc = term if acc is None else acc + term
    out_ref[0] = acc.astype(BF16)


def _branch_mix(o_a, o_m, o_s, u, w_branch, b_gate3):
    g, m, _ = o_a.shape
    tm = min(1024, m)
    d = D_MODEL
    o_spec = pl.BlockSpec((1, tm, BRANCH_W), lambda b, i, j: (b, i, 0))

    def gate_spec(n):
        base = (U_GATE + n * d) // MIX_TN
        return pl.BlockSpec((1, tm, MIX_TN), lambda b, i, j: (b, i, base + j))

    return pl.pallas_call(
        _branch_mix_kernel,
        grid=(g, m // tm, d // MIX_TN),
        in_specs=[o_spec, o_spec, o_spec,
                  pl.BlockSpec((3, BRANCH_W, MIX_TN), lambda b, i, j: (0, 0, j)),
                  gate_spec(0), gate_spec(1), gate_spec(2),
                  pl.BlockSpec((3, MIX_TN), lambda b, i, j: (0, j))],
        out_specs=pl.BlockSpec((1, tm, MIX_TN), lambda b, i, j: (b, i, j)),
        out_shape=jax.ShapeDtypeStruct((g, m, d), BF16),
        compiler_params=_cparams("arbitrary", "arbitrary", "arbitrary"),
        name="branch_mix",
    )(o_a, o_m, o_s, w_branch, u, u, u, b_gate3)


def _out_proj_kernel(mix_ref, w_ref, x_ref, mod_ref, g_ref, out_ref):
    y = _dot(mix_ref[0], w_ref[...])
    yn = y * lax.rsqrt(jnp.mean(y * y, axis=-1, keepdims=True) + EPS) * g_ref[...]
    out_ref[0] = x_ref[0] + mod_ref[0, 2:3, :] * yn


def _out_proj(mixed, w_out, x3, mod3, g_post):
    g, m, d = x3.shape
    tm = min(512, m)
    return pl.pallas_call(
        _out_proj_kernel,
        grid=(g, m // tm),
        in_specs=[pl.BlockSpec((1, tm, d), lambda b, i: (b, i, 0)),
                  pl.BlockSpec((d, d), lambda b, i: (0, 0)),
                  pl.BlockSpec((1, tm, d), lambda b, i: (b, i, 0)),
                  pl.BlockSpec((1, 3, d), lambda b, i: (b, 0, 0)),
                  pl.BlockSpec((1, d), lambda b, i: (0, 0))],
        out_specs=pl.BlockSpec((1, tm, d), lambda b, i: (b, i, 0)),
        out_shape=jax.ShapeDtypeStruct((g, m, d), F32),
        compiler_params=_cparams("arbitrary", "arbitrary"),
        name="out_proj",
    )(mixed, w_out, x3, mod3, g_post.reshape(1, d))


def _pack_in_weights(w_in, w_gate):
    o = 0
    cols = {}
    for name, w in (("att_q", BRANCH_W), ("att_k", ATT_KV_W), ("att_v", ATT_KV_W), ("att_z", BRANCH_W),
                    ("ml_q", BRANCH_W), ("ml_k", BRANCH_W), ("ml_v", BRANCH_W), ("ml_o", BRANCH_W),
                    ("ml_z", BRANCH_W), ("ml_gates", 4 * ML_HEADS), ("ssm_xbc", SSM_CONV_CH),
                    ("ssm_dt", 2 * SSM_HEADS), ("ssm_z", BRANCH_W)):
        cols[name] = w_in[:, o:o + w]
        o += w
    perm = np.concatenate([np.arange(0, ATT_HD, 2), np.arange(1, ATT_HD, 2)])
    perm_q = (np.arange(ATT_HEADS)[:, None] * ATT_HD + perm[None, :]).reshape(-1)
    perm_k = (np.arange(ATT_KV_HEADS)[:, None] * ATT_HD + perm[None, :]).reshape(-1)
    big = jnp.concatenate([
        cols["ml_k"] * (ML_HD ** -0.5), cols["ml_v"], cols["ssm_xbc"],
        cols["att_k"][:, perm_k], cols["att_v"], cols["att_q"][:, perm_q], cols["att_z"],
        cols["ml_q"], cols["ml_o"], cols["ml_z"], cols["ssm_z"], w_gate], axis=1)
    mg = cols["ml_gates"].reshape(-1, 4, ML_HEADS)
    zpad = lambda w: jnp.zeros((w_in.shape[0], w), BF16)
    small = jnp.concatenate([
        mg[:, 0], mg[:, 2], zpad(LANES - 2 * ML_HEADS),
        mg[:, 1], mg[:, 3], zpad(LANES - 2 * ML_HEADS),
        cols["ssm_dt"], zpad(LANES - 2 * SSM_HEADS)], axis=1)
    return big, small


def _rope_tables(seq):
    rows = seq // GRID_W
    row = jnp.repeat(jnp.arange(rows), GRID_W).astype(F32)
    col = jnp.tile(jnp.arange(GRID_W), rows).astype(F32)
    n_freq = ATT_HD // 4
    inv = ROPE_BASE ** (-jnp.arange(n_freq, dtype=F32) / n_freq)
    ang = jnp.concatenate([row[:, None] * inv, col[:, None] * inv], -1)
    cos2 = jnp.concatenate([jnp.cos(ang), jnp.cos(ang)], -1)
    sin2 = jnp.concatenate([-jnp.sin(ang), jnp.sin(ang)], -1)
    q_scale = ATT_SCALE * LOG2E
    return cos2 * q_scale, sin2 * q_scale, cos2, sin2


def _lane_row(v):
    v = v.astype(F32).reshape(1, -1)
    return jnp.pad(v, ((0, 0), (0, LANES - v.shape[1])))


def _ssd_params(dt_bias, a_log, d_skip, norm_g):
    chan_head = np.arange(BRANCH_W) // SSM_HD
    e = np.zeros((2, LANES, BRANCH_W), np.float32)
    for d in range(2):
        e[d, SSM_HEADS * d + chan_head, np.arange(BRANCH_W)] = 1.0
    even = ((chan_head % 2) == 0).astype(np.float32).reshape(1, BRANCH_W)
    dsk = jnp.repeat(d_skip.astype(F32), SSM_HD).reshape(1, BRANCH_W)
    return (_lane_row(dt_bias), _lane_row(-jnp.exp(a_log.astype(F32))), jnp.asarray(e, BF16),
            jnp.asarray(even), dsk, norm_g.astype(F32).reshape(1, BRANCH_W))


def kernel(x, c, ctx, c_ctx, w_mod, b_mod, g_pre, w_in, w_gate, b_gate, att_sink, ml_gate_bias,
           ml_norm, ssm_conv_w, ssm_conv_b, ssm_dt_bias, ssm_a_log, ssm_d, ssm_norm, w_branch,
           w_out, g_post):
    b, s, d = x.shape
    lc = ctx.shape[1]
    depth = w_mod.shape[0]
    rows = -(-(b + 1) // 8) * 8
    cc = jnp.concatenate([c, c_ctx[None, :], jnp.zeros((rows - b - 1, d), F32)], axis=0)
    mod = _modulation(cc, w_mod, b_mod)
    tables = _rope_tables(s)
    xc = ctx.reshape(1, b * lc, d)
    for l in range(depth):
        need_ctx = l < depth - 1
        w_big, w_small = _pack_in_weights(w_in[l].astype(BF16), w_gate[l].astype(BF16))
        mod_x = mod[l, :b].reshape(b, 3, d)
        mod_c = mod[l, b:b + 1].reshape(1, 3, d)
        u, us = _in_proj(x, mod_x, g_pre[l], w_big, w_small)
        uc, usc = _in_proj(xc, mod_c, g_pre[l], w_big, w_small,
                           U_COLS if need_ctx else U_STATE_COLS)
        uc = uc.reshape(b, lc, -1)
        usc = usc.reshape(b, lc, -1)

        sink = att_sink[l].astype(F32)
        o_a = _attention(u, uc, sink, tables)

        gb = ml_gate_bias[l]
        gbias = jnp.concatenate([_lane_row(jnp.concatenate([gb[0], gb[2]])),
                                 _lane_row(jnp.concatenate([gb[1], gb[3]]))], axis=1)
        st0 = (jnp.zeros((b, 2 * ML_HEADS, ML_HD, ML_CW), F32), jnp.zeros((b, 8, LANES), F32))
        sp = _ssd_params(ssm_dt_bias[l], ssm_a_log[l], ssm_d[l], ssm_norm[l])
        h0 = jnp.zeros((b, 2 * SSM_GROUPS, SSM_STATE, SSM_HPG * SSM_HD), F32)
        xbc_c = _ssm_conv(uc, ssm_conv_w[l], ssm_conv_b[l])
        if need_ctx:
            o_m_c, st = _mlstm(uc, usc, gbias, ml_norm[l].astype(F32), st0)
            o_s_c, hst = _ssd(xbc_c, uc, usc, sp, h0)
        else:
            st = _mlstm_state(uc, usc, gbias, st0)
            hst = _ssd_state(xbc_c, usc, sp, h0)
        o_m, _ = _mlstm(u, us, gbias, ml_norm[l].astype(F32), st)
        o_s, _ = _ssd(_ssm_conv(u, ssm_conv_w[l], ssm_conv_b[l]), u, us, sp, hst)

        wb = w_branch[l].astype(BF16)
        bg = b_gate[l].astype(F32).reshape(3, d)
        wo = w_out[l].astype(BF16)
        mixed = _branch_mix(o_a, o_m, o_s, u, wb, bg)
        x_new = _out_proj(mixed, wo, x, mod_x, g_post[l])
        if need_ctx:
            o_a_c = _ctx_attention(uc, sink)
            flat = lambda a: a.reshape(1, b * lc, a.shape[-1])
            mixed_c = _branch_mix(flat(o_a_c), flat(o_m_c), flat(o_s_c), flat(uc), wb, bg)
            xc = _out_proj(mixed_c, wo, xc, mod_c, g_post[l])
        x = x_new
    return x
```
